```python
import math
import jax, jax.numpy as jnp
from jax import lax
import numpy as np

D_MODEL = 1024
BATCH = 2
SEQ = 16384
DEPTH = 2

N_A_LAYERS = DEPTH // 2
N_B_LAYERS = DEPTH - N_A_LAYERS

SSM_EXPAND = 2
D_INNER = SSM_EXPAND * D_MODEL
SSM_HEAD_DIM = 64
SSM_HEADS = D_INNER // SSM_HEAD_DIM
SSM_GROUPS = 8
SSM_HEADS_PER_GROUP = SSM_HEADS // SSM_GROUPS
D_STATE = 128
D_CONV = 4
CHUNK = 128
CONV_DIM = D_INNER + 2 * SSM_GROUPS * D_STATE
IN_PROJ_DIM = 2 * D_INNER + 2 * SSM_GROUPS * D_STATE + SSM_HEADS
GATED_NORM_GROUP = D_INNER // SSM_GROUPS

SB_HEADS = 16
SB_HEAD_DIM = D_MODEL // SB_HEADS
Q_BLOCK = 128

D_FF = 4 * D_MODEL

EPS = 1e-5

kernel_name = "yoco_mamba2_stickbreaking_hybrid"


def rms_norm(x, g):
    xf = x.astype(jnp.float32)
    y = xf * lax.rsqrt(jnp.mean(xf * xf, axis=-1, keepdims=True) + EPS)
    return (y * g.astype(jnp.float32)).astype(x.dtype)


def causal_depthwise_conv(x, w, b):
    c = x.shape[-1]
    y = lax.conv_general_dilated(
        x, w[:, None, :].astype(x.dtype), window_strides=(1,), padding=[(D_CONV - 1, 0)],
        dimension_numbers=("NWC", "WIO", "NWC"), feature_group_count=c)
    return y + b


def ssd_chunked(x, dt, a_neg, bm, cm):
    b, s = x.shape[0], x.shape[1]
    nc = s // CHUNK
    xdt = x * dt[..., None]
    a = dt * a_neg

    def to_chunks(t):
        return jnp.moveaxis(t.reshape((b, nc, CHUNK) + t.shape[2:]), 1, 0)

    tril = jnp.arange(CHUNK)[:, None] >= jnp.arange(CHUNK)[None, :]

    def body(state, inp):
        xdt_c, a_c, b_c, c_c = inp
        ac = jnp.moveaxis(jnp.cumsum(a_c, axis=1), 1, -1)
        seg = ac[..., :, None] - ac[..., None, :]
        decay = jnp.exp(jnp.where(tril, seg, -jnp.inf))
        cb = jnp.einsum("blgn,bsgn->bgls", c_c, b_c)
        y_intra = jnp.einsum("bgrls,bsgrp->blgrp", cb[:, :, None] * decay, xdt_c)
        y_inter = jnp.einsum("blgn,bgrpn,bgrl->blgrp", c_c, state, jnp.exp(ac))
        decay_to_end = jnp.exp(ac[..., -1:] - ac)
        new_state = (state * jnp.exp(ac[..., -1])[..., None, None]
                     + jnp.einsum("bsgn,bgrs,bsgrp->bgrpn", b_c, decay_to_end, xdt_c))
        return new_state, y_intra + y_inter

    state0 = jnp.zeros((b, SSM_GROUPS, SSM_HEADS_PER_GROUP, SSM_HEAD_DIM, D_STATE), x.dtype)
    _, ys = lax.scan(body, state0, (to_chunks(xdt), to_chunks(a), to_chunks(bm), to_chunks(cm)))
    return jnp.moveaxis(ys, 0, 1).reshape(x.shape)


def mamba2_mixer(u, w_in, conv_w, conv_b, dt_bias, a_log, d_skip, gnorm_g, w_out):
    b, s, _ = u.shape
    zxbcdt = u @ w_in
    z = zxbcdt[..., :D_INNER]
    xbc = zxbcdt[..., D_INNER:D_INNER + CONV_DIM]
    dt_raw = zxbcdt[..., D_INNER + CONV_DIM:]
    xbc = jax.nn.silu(causal_depthwise_conv(xbc, conv_w, conv_b))
    xs = xbc[..., :D_INNER].reshape(b, s, SSM_GROUPS, SSM_HEADS_PER_GROUP, SSM_HEAD_DIM)
    bm = xbc[..., D_INNER:D_INNER + SSM_GROUPS * D_STATE].reshape(b, s, SSM_GROUPS, D_STATE)
    cm = xbc[..., D_INNER + SSM_GROUPS * D_STATE:].reshape(b, s, SSM_GROUPS, D_STATE)
    dt = jax.nn.softplus(dt_raw + dt_bias).reshape(b, s, SSM_GROUPS, SSM_HEADS_PER_GROUP)
    a_neg = (-jnp.exp(a_log)).reshape(SSM_GROUPS, SSM_HEADS_PER_GROUP)
    y = ssd_chunked(xs, dt, a_neg, bm, cm)
    y = y + d_skip.reshape(SSM_GROUPS, SSM_HEADS_PER_GROUP)[:, :, None] * xs
    y = y.reshape(b, s, D_INNER) * jax.nn.silu(z)
    y = rms_norm(y.reshape(b, s, SSM_GROUPS, GATED_NORM_GROUP), jnp.ones((GATED_NORM_GROUP,), y.dtype))
    y = y.reshape(b, s, D_INNER) * gnorm_g
    return y @ w_out


def stick_breaking_attention(q, k, v):
    b, h, s, d = q.shape
    nb = s // Q_BLOCK
    scale = 1.0 / math.sqrt(d)
    idx = jnp.arange(Q_BLOCK)
    u_incl = (idx[:, None] >= idx[None, :]).astype(jnp.float32)
    kb_all = k.reshape(b, h, nb, Q_BLOCK, d)
    vb_all = v.reshape(b, h, nb, Q_BLOCK, d)
    outs = []
    for i in range(nb):
        nk = i + 1
        qb = q[:, :, i * Q_BLOCK:(i + 1) * Q_BLOCK]
        kb = kb_all[:, :, :nk]
        vb = vb_all[:, :, :nk]
        z = jnp.einsum("bhtd,bhnsd->bhtns", qb, kb).astype(jnp.float32) * scale
        q_pos = i * Q_BLOCK + idx
        k_pos = (jnp.arange(nk)[:, None] * Q_BLOCK + idx[None, :])
        mask = k_pos[None, :, :] < q_pos[:, None, None]
        log1m = jnp.where(mask, -jax.nn.softplus(z), 0.0)
        within = jnp.einsum("bhtnj,jr->bhtnr", log1m, u_incl)
        blk_tot = within[..., 0]
        nidx = jnp.arange(nk)
        later_mat = (nidx[:, None] > nidx[None, :]).astype(jnp.float32)
        later = jnp.einsum("bhtm,mn->bhtn", blk_tot, later_mat)
        logw = z + within + later[..., None]
        w = jnp.exp(jnp.where(mask, logw, -jnp.inf))
        outs.append(jnp.einsum("bhtns,bhnsd->bthd", w.astype(v.dtype), vb))
    out = jnp.concatenate(outs, axis=1)
    return out.reshape(b, s, h * d)


def split_heads(t):
    b, s, _ = t.shape
    return t.reshape(b, s, SB_HEADS, SB_HEAD_DIM).transpose(0, 2, 1, 3)


def sqrelu_mlp(u, w_up, w_down):
    return jnp.square(jax.nn.relu(u @ w_up)) @ w_down


def setup_inputs(seed: int = 0) -> dict:
    key = jax.random.key(seed)
    ks = jax.random.split(key, 24)
    f32 = jnp.float32

    def nrm(k, shape, fan_in):
        return jax.random.normal(k, shape, f32) * (fan_in ** -0.5)

    def gain(k, shape):
        return 1.0 + 0.02 * jax.random.normal(k, shape, f32)

    dt0 = jnp.exp(jax.random.uniform(ks[5], (N_A_LAYERS, SSM_HEADS), f32)
                  * (math.log(0.1) - math.log(0.001)) + math.log(0.001))
    return {
        "x": jax.random.normal(ks[0], (BATCH, SEQ, D_MODEL), f32),
        "a_norm_g": gain(ks[1], (N_A_LAYERS, D_MODEL)),
        "a_w_in": nrm(ks[2], (N_A_LAYERS, D_MODEL, IN_PROJ_DIM), D_MODEL),
        "a_conv_w": nrm(ks[3], (N_A_LAYERS, D_CONV, CONV_DIM), D_CONV),
        "a_conv_b": 0.02 * jax.random.normal(ks[4], (N_A_LAYERS, CONV_DIM), f32),
        "a_dt_bias": dt0 + jnp.log(-jnp.expm1(-dt0)),
        "a_a_log": jnp.log(jax.random.uniform(ks[6], (N_A_LAYERS, SSM_HEADS), f32, 1.0, 16.0)),
        "a_d_skip": gain(ks[7], (N_A_LAYERS, SSM_HEADS)),
        "a_gnorm_g": gain(ks[8], (N_A_LAYERS, D_INNER)),
        "a_w_out": nrm(ks[9], (N_A_LAYERS, D_INNER, D_MODEL), D_INNER),
        "kv_norm_g": gain(ks[10], (D_MODEL,)),
        "w_kv": nrm(ks[11], (D_MODEL, 2 * D_MODEL), D_MODEL),
        "k_norm_g": gain(ks[12], (SB_HEAD_DIM,)),
        "b_norm_g": gain(ks[13], (N_B_LAYERS, D_MODEL)),
        "b_w_q": nrm(ks[14], (N_B_LAYERS, D_MODEL, D_MODEL), D_MODEL),
        "b_q_norm_g": gain(ks[15], (N_B_LAYERS, SB_HEAD_DIM)),
        "b_w_o": nrm(ks[16], (N_B_LAYERS, D_MODEL, D_MODEL), D_MODEL),
        "mlp_norm_g": gain(ks[17], (DEPTH, D_MODEL)),
        "w_up": nrm(ks[18], (DEPTH, D_MODEL, D_FF), D_MODEL),
        "w_down": nrm(ks[19], (DEPTH, D_FF, D_MODEL), D_FF),
    }


def reference(x, a_norm_g, a_w_in, a_conv_w, a_conv_b, a_dt_bias, a_a_log, a_d_skip, a_gnorm_g,
              a_w_out, kv_norm_g, w_kv, k_norm_g, b_norm_g, b_w_q, b_q_norm_g, b_w_o,
              mlp_norm_g, w_up, w_down):
    h = x
    k = v = None
    for l in range(DEPTH):
        if l < N_A_LAYERS:
            u = rms_norm(h, a_norm_g[l])
            h = h + mamba2_mixer(u, a_w_in[l], a_conv_w[l], a_conv_b[l], a_dt_bias[l],
                                 a_a_log[l], a_d_skip[l], a_gnorm_g[l], a_w_out[l])
        else:
            j = l - N_A_LAYERS
            if j == 0:
                kv = rms_norm(h, kv_norm_g) @ w_kv
                k = rms_norm(split_heads(kv[..., :D_MODEL]), k_norm_g)
                v = split_heads(kv[..., D_MODEL:])
            u = rms_norm(h, b_norm_g[j])
            q = rms_norm(split_heads(u @ b_w_q[j]), b_q_norm_g[j])
            h = h + stick_breaking_attention(q, k, v) @ b_w_o[j]
        h = h + sqrelu_mlp(rms_norm(h, mlp_norm_g[l]), w_up[l], w_down[l])
    return h
```

```python
import functools
import math

import jax
import jax.numpy as jnp
from jax import lax
from jax.experimental import pallas as pl
from jax.experimental.pallas import tpu as pltpu

F32 = jnp.float32
BF16 = jnp.bfloat16
EPS = 1e-5

SSM_HEAD_DIM = 64
SSM_GROUPS = 8
D_STATE = 128
D_CONV = 4
SB_HEAD_DIM = 64

LANES = 128
SUBLANES = 8
VMEM_LIMIT_BYTES = 56 * 1024 * 1024

ROW_TILE = 512
SSD_CHUNK = 256
CONV_COL_TILE = 512
FF_TILE = 1024
ATT_TILE = 256
ATT_DEAD_LOGDECAY = 105.0


def _dot(a, b):
    return jnp.dot(a, b, preferred_element_type=F32)


def _dot_nt(a, b):
    return lax.dot_general(a, b, (((1,), (1,)), ((), ())), preferred_element_type=F32)


def _softplus(x):
    return jnp.maximum(x, 0.0) + jnp.log1p(jnp.exp(-jnp.abs(x)))


def _rms_scale(x):
    return lax.rsqrt(jnp.mean(x * x, axis=-1, keepdims=True) + EPS)


def _resident(shape):
    zeros = (0,) * len(shape)
    return pl.BlockSpec(shape, lambda *_: zeros, pipeline_mode=pl.Buffered(1))


def _params(*semantics):
    return pltpu.CompilerParams(dimension_semantics=semantics, vmem_limit_bytes=VMEM_LIMIT_BYTES)


def _inproj_kernel(x_ref, g_ref, w_ref, wdt_ref, zx_ref, dt_ref, *, col_tile):
    x = x_ref[...]
    u = (x * _rms_scale(x) * g_ref[...]).astype(BF16)
    for c in range(w_ref.shape[1] // col_tile):
        sl = slice(c * col_tile, (c + 1) * col_tile)
        zx_ref[:, sl] = _dot(u, w_ref[:, sl]).astype(BF16)
    dt_ref[...] = _dot(u, wdt_ref[...])


def _inproj(h, g, w_main, w_dt):
    t, d = h.shape
    n = w_main.shape[1]
    return pl.pallas_call(
        functools.partial(_inproj_kernel, col_tile=1024),
        out_shape=(jax.ShapeDtypeStruct((t, n), BF16), jax.ShapeDtypeStruct((t, LANES), F32)),
        grid=(t // ROW_TILE,),
        in_specs=[pl.BlockSpec((ROW_TILE, d), lambda i: (i, 0)),
                  _resident((1, d)), _resident((d, n)), _resident((d, LANES))],
        out_specs=(pl.BlockSpec((ROW_TILE, n), lambda i: (i, 0)),
                   pl.BlockSpec((ROW_TILE, LANES), lambda i: (i, 0))),
        compiler_params=_params("parallel"),
        name="in_proj",
    )(h, g, w_main, w_dt)


def _ssd_kernel(h_ref, z_ref, x_ref, bc_ref, dt_ref, convw_ref, convb_ref, dtb_ref, alog_ref,
                dskip_ref, gn_ref, wout_ref, out_ref, cbuf, act, state, ybuf):
    L = SSD_CHUNK
    d_inner = x_ref.shape[1]
    n_heads = d_inner // SSM_HEAD_DIM
    heads_per_group = n_heads // SSM_GROUPS
    gw = heads_per_group * SSM_HEAD_DIM
    conv_dim = cbuf.shape[1]

    @pl.when(pl.program_id(1) == 0)
    def _():
        cbuf[0:SUBLANES, :] = jnp.zeros((SUBLANES, conv_dim), F32)
        state[...] = jnp.zeros(state.shape, F32)

    cbuf[SUBLANES:SUBLANES + L, 0:d_inner] = x_ref[...].astype(F32)
    cbuf[SUBLANES:SUBLANES + L, d_inner:conv_dim] = bc_ref[...].astype(F32)
    for c in range(conv_dim // CONV_COL_TILE):
        sl = slice(c * CONV_COL_TILE, (c + 1) * CONV_COL_TILE)
        acc = convb_ref[:, sl] + convw_ref[D_CONV - 1:D_CONV, sl] * cbuf[SUBLANES:SUBLANES + L, sl]
        for k in range(D_CONV - 1):
            r0 = SUBLANES - (D_CONV - 1) + k
            acc = acc + convw_ref[k:k + 1, sl] * cbuf[r0:r0 + L, sl]
        act[:, sl] = acc * jax.nn.sigmoid(acc)
    cbuf[0:SUBLANES, :] = cbuf[L:L + SUBLANES, :]

    dt = _softplus(dt_ref[...] + dtb_ref[...])
    a = dt * (-jnp.exp(alog_ref[...]))
    row = lax.broadcasted_iota(jnp.int32, (L, L), 0)
    col = lax.broadcasted_iota(jnp.int32, (L, L), 1)
    lower = col <= row
    tri = jnp.where(lower, 1.0, 0.0).astype(BF16)
    a_hi = a.astype(BF16)
    r1 = a - a_hi.astype(F32)
    a_mid = r1.astype(BF16)
    a_lo = (r1 - a_mid.astype(F32)).astype(BF16)
    cum = _dot(tri, a_hi) + _dot(tri, a_mid) + _dot(tri, a_lo)
    cum_t = cum.T
    ecum = jnp.exp(cum)
    to_end = jnp.exp(cum[L - 1:L, :] - cum)

    for g in range(SSM_GROUPS):
        b_g = act[:, d_inner + g * D_STATE:d_inner + (g + 1) * D_STATE]
        c_off = d_inner + SSM_GROUPS * D_STATE
        c_g = act[:, c_off + g * D_STATE:c_off + (g + 1) * D_STATE].astype(BF16)
        cb = _dot_nt(c_g, b_g.astype(BF16))
        st = state[g]
        inter = _dot(c_g, st.astype(BF16))
        ys, xws, sdec = [], [], []
        for j in range(heads_per_group):
            hd = g * heads_per_group + j
            hs = slice(hd * SSM_HEAD_DIM, (hd + 1) * SSM_HEAD_DIM)
            xh = act[:, hs]
            xdt = xh * dt[:, hd:hd + 1]
            seg = cum[:, hd:hd + 1] - cum_t[hd:hd + 1, :]
            decay = jnp.exp(jnp.where(lower, seg, -jnp.inf))
            y = _dot((cb * decay).astype(BF16), xdt.astype(BF16))
            y = y + inter[:, j * SSM_HEAD_DIM:(j + 1) * SSM_HEAD_DIM] * ecum[:, hd:hd + 1]
            ys.append(y + dskip_ref[:, hs] * xh)
            xws.append((xdt * to_end[:, hd:hd + 1]).astype(BF16))
            sdec.append(jnp.broadcast_to(ecum[L - 1:L, hd:hd + 1], (1, SSM_HEAD_DIM)))
        y_g = jnp.concatenate(ys, axis=1)
        xw_g = jnp.concatenate(xws, axis=1)
        state[g] = st * jnp.concatenate(sdec, axis=1) + _dot(b_g.T.astype(BF16), xw_g)
        gs = slice(g * gw, (g + 1) * gw)
        zg = z_ref[:, gs].astype(F32)
        yg = y_g * (zg * jax.nn.sigmoid(zg))
        ybuf[:, gs] = (yg * _rms_scale(yg) * gn_ref[:, gs]).astype(BF16)

    out_ref[...] = h_ref[...] + _dot(ybuf[...], wout_ref[...])


def _ssd(h, zx, dt_raw, conv_w, conv_b, dt_bias, a_log, d_skip, gnorm_g, w_out, batch):
    t, d = h.shape
    d_inner = w_out.shape[0]
    conv_dim = conv_w.shape[1]
    assert conv_dim == 2 * d_inner, "x and B|C column blocks are addressed with one block width"
    L = SSD_CHUNK
    chunks = t // batch // L
    n_heads = d_inner // SSM_HEAD_DIM
    gw = d_inner // SSM_GROUPS
    rows = lambda b, c: (b * chunks + c, 0)
    return pl.pallas_call(
        _ssd_kernel,
        out_shape=jax.ShapeDtypeStruct((t, d), F32),
        grid=(batch, chunks),
        in_specs=[pl.BlockSpec((L, d), rows),
                  pl.BlockSpec((L, d_inner), lambda b, c: (b * chunks + c, 0)),
                  pl.BlockSpec((L, d_inner), lambda b, c: (b * chunks + c, 1)),
                  pl.BlockSpec((L, d_inner), lambda b, c: (b * chunks + c, 2)),
                  pl.BlockSpec((L, LANES), rows),
                  _resident((D_CONV, conv_dim)), _resident((1, conv_dim)),
                  _resident((1, LANES)), _resident((1, LANES)),
                  _resident((1, d_inner)), _resident((1, d_inner)), _resident((d_inner, d))],
        out_specs=pl.BlockSpec((L, d), rows),
        scratch_shapes=[pltpu.VMEM((L + SUBLANES, conv_dim), F32),
                        pltpu.VMEM((L, conv_dim), F32),
                        pltpu.VMEM((SSM_GROUPS, D_STATE, gw), F32),
                        pltpu.VMEM((L, d_inner), BF16)],
        compiler_params=_params("arbitrary", "arbitrary"),
        name="ssd",
    )(h, zx, zx, zx, dt_raw, conv_w, conv_b, dt_bias, a_log, d_skip, gnorm_g, w_out)


def _mlp_body(h, g_ref, wup_ref, wdown_ref, out_ref):
    u = (h * _rms_scale(h) * g_ref[...]).astype(BF16)
    acc = h
    for c in range(wup_ref.shape[1] // FF_TILE):
        sl = slice(c * FF_TILE, (c + 1) * FF_TILE)
        a = jnp.maximum(_dot(u, wup_ref[:, sl]), 0.0)
        acc = acc + _dot((a * a).astype(BF16), wdown_ref[sl, :])
    out_ref[...] = acc


def _mlp_kernel(h_ref, g_ref, wup_ref, wdown_ref, out_ref):
    _mlp_body(h_ref[...], g_ref, wup_ref, wdown_ref, out_ref)


def _proj_mlp_kernel(h_ref, a_ref, wo_ref, g_ref, wup_ref, wdown_ref, out_ref):
    _mlp_body(h_ref[...] + _dot(a_ref[...], wo_ref[...]), g_ref, wup_ref, wdown_ref, out_ref)


def _mlp(h, g, w_up, w_down, attn=None, w_o=None):
    t, d = h.shape
    ff = w_up.shape[1]
    rows = pl.BlockSpec((ROW_TILE, d), lambda i: (i, 0))
    weights = [_resident((1, d)), _resident((d, ff)), _resident((ff, d))]
    if attn is None:
        body, ins, specs = _mlp_kernel, (h, g, w_up, w_down), [rows] + weights
    else:
        body, ins = _proj_mlp_kernel, (h, attn, w_o, g, w_up, w_down)
        specs = [rows, rows, _resident(w_o.shape)] + weights
    return pl.pallas_call(
        body,
        out_shape=jax.ShapeDtypeStruct((t, d), F32),
        grid=(t // ROW_TILE,),
        in_specs=specs,
        out_specs=rows,
        compiler_params=_params("parallel"),
        name="mlp",
    )(*ins)


def _kvq_kernel(h_ref, gkv_ref, gq_ref, wkv_ref, wq_ref, gk_ref, q_ref, kt_ref, v_ref):
    x = h_ref[0]
    d = x.shape[1]
    xn = x * _rms_scale(x)
    ukv = (xn * gkv_ref[...]).astype(BF16)
    uq = (xn * gq_ref[...]).astype(BF16)
    q_ref[0] = _dot(uq, wq_ref[...]).astype(BF16)
    v_ref[0] = _dot(ukv, wkv_ref[:, d:2 * d]).astype(BF16)
    k_t = _dot(ukv, wkv_ref[:, 0:d]).T
    for hd in range(d // SB_HEAD_DIM):
        hs = slice(hd * SB_HEAD_DIM, (hd + 1) * SB_HEAD_DIM)
        kh = k_t[hs, :]
        scale = lax.rsqrt(jnp.mean(kh * kh, axis=0, keepdims=True) + EPS)
        kt_ref[0, hs, :] = (kh * scale * gk_ref[...]).astype(BF16)


def _kvq(h3, g_kv, g_q, w_kv, w_q, g_k):
    b, s, d = h3.shape
    rows = pl.BlockSpec((1, ROW_TILE, d), lambda bi, i: (bi, i, 0))
    return pl.pallas_call(
        _kvq_kernel,
        out_shape=(jax.ShapeDtypeStruct((b, s, d), BF16),
                   jax.ShapeDtypeStruct((b, d, s), BF16),
                   jax.ShapeDtypeStruct((b, s, d), BF16)),
        grid=(b, s // ROW_TILE),
        in_specs=[rows, _resident((1, d)), _resident((1, d)), _resident((d, 2 * d)),
                  _resident((d, d)), _resident((SB_HEAD_DIM, 1))],
        out_specs=(rows, pl.BlockSpec((1, d, ROW_TILE), lambda bi, i: (bi, 0, i)), rows),
        compiler_params=_params("parallel", "parallel"),
        name="kvq",
    )(h3, g_kv, g_q, w_kv, w_q, g_k)


def _attn_kernel(q_ref, kt_ref, v_ref, gq_ref, o_ref):
    T = ATT_TILE
    qi = pl.program_id(2)
    lane = lax.broadcasted_iota(jnp.int32, (T, LANES), 1)
    first = lane < SB_HEAD_DIM
    q = q_ref[0].astype(F32)
    sq = q * q
    ms0 = jnp.sum(jnp.where(first, sq, 0.0), axis=-1, keepdims=True) / SB_HEAD_DIM
    ms1 = jnp.sum(jnp.where(first, 0.0, sq), axis=-1, keepdims=True) / SB_HEAD_DIM
    inv = jnp.where(first, lax.rsqrt(ms0 + EPS), lax.rsqrt(ms1 + EPS))
    qn = q * inv * gq_ref[...] * (1.0 / math.sqrt(SB_HEAD_DIM))

    row = lax.broadcasted_iota(jnp.int32, (T, T), 0)
    col = lax.broadcasted_iota(jnp.int32, (T, T), 1)
    suffix = jnp.where(row >= col, 1.0, 0.0).astype(BF16)
    causal = col < row

    def tile(qm, kj, later, masked):
        start = pl.multiple_of(kj * T, T)
        z = _dot(qm, kt_ref[0, :, pl.ds(start, T)])
        sp = _softplus(z)
        if masked:
            sp = jnp.where(causal, sp, 0.0)
        cum = _dot(sp.astype(BF16), suffix)
        p = jnp.exp(z - cum)
        if masked:
            p = jnp.where(causal, p, 0.0)
        pv = _dot(p.astype(BF16), v_ref[0, pl.ds(start, T), :])
        return pv * jnp.exp(-later), later + cum[:, 0:1]

    def head(qm):
        acc, later = tile(qm, qi, jnp.zeros((T, 1), F32), True)

        def cond(carry):
            kj, _, _, live = carry
            return jnp.logical_and(kj >= 0, live < ATT_DEAD_LOGDECAY)

        def body(carry):
            kj, acc, later, _ = carry
            contrib, later = tile(qm, kj, later, False)
            return kj - 1, acc + contrib, later, jnp.min(later)

        _, acc, _, _ = lax.while_loop(cond, body, (qi - 1, acc, later, jnp.min(later)))
        return acc

    acc0 = head(jnp.where(first, qn, 0.0).astype(BF16))
    acc1 = head(jnp.where(first, 0.0, qn).astype(BF16))
    o_ref[0] = jnp.where(first, acc0, acc1).astype(BF16)


def _attention(q, k_t, v, g_q2):
    b, s, d = q.shape
    T = ATT_TILE
    return pl.pallas_call(
        _attn_kernel,
        out_shape=jax.ShapeDtypeStruct((b, s, d), BF16),
        grid=(b, d // LANES, s // T),
        in_specs=[pl.BlockSpec((1, T, LANES), lambda bi, hp, i: (bi, i, hp)),
                  pl.BlockSpec((1, LANES, s), lambda bi, hp, i: (bi, hp, 0)),
                  pl.BlockSpec((1, s, LANES), lambda bi, hp, i: (bi, 0, hp)),
                  _resident((1, LANES))],
        out_specs=pl.BlockSpec((1, T, LANES), lambda bi, hp, i: (bi, i, hp)),
        compiler_params=_params("parallel", "parallel", "arbitrary"),
        name="attn",
    )(q, k_t, v, g_q2)


def kernel(x, a_norm_g, a_w_in, a_conv_w, a_conv_b, a_dt_bias, a_a_log, a_d_skip, a_gnorm_g, a_w_out,
           kv_norm_g, w_kv, k_norm_g, b_norm_g, b_w_q, b_q_norm_g, b_w_o, mlp_norm_g, w_up, w_down):
    b, s, d = x.shape
    n_a, n_b = a_w_in.shape[0], b_w_q.shape[0]
    d_inner = a_w_out.shape[1]
    conv_dim = a_conv_w.shape[2]
    n_heads = a_dt_bias.shape[1]
    main = d_inner + conv_dim

    def lane_pad(v):
        return jnp.pad(v, (0, LANES - v.shape[0]))[None, :]

    h = x.reshape(b * s, d)
    for l in range(n_a):
        w_in = a_w_in[l]
        w_dt = jnp.pad(w_in[:, main:], ((0, 0), (0, LANES - n_heads))).astype(BF16)
        zx, dt_raw = _inproj(h, a_norm_g[l][None, :], w_in[:, :main].astype(BF16), w_dt)
        h = _ssd(h, zx, dt_raw, a_conv_w[l], a_conv_b[l][None, :], lane_pad(a_dt_bias[l]),
                 lane_pad(a_a_log[l]), jnp.repeat(a_d_skip[l], SSM_HEAD_DIM)[None, :],
                 a_gnorm_g[l][None, :], a_w_out[l].astype(BF16), b)
        h = _mlp(h, mlp_norm_g[l][None, :], w_up[l].astype(BF16), w_down[l].astype(BF16))
    q = k_t = v = None
    for j in range(n_b):
        l = n_a + j
        if j == 0:
            q, k_t, v = _kvq(h.reshape(b, s, d), kv_norm_g[None, :], b_norm_g[j][None, :],
                             w_kv.astype(BF16), b_w_q[j].astype(BF16), k_norm_g[:, None])
        else:
            raise NotImplementedError("one stick-breaking layer per shared K/V projection call")
        attn = _attention(q, k_t, v, jnp.tile(b_q_norm_g[j], LANES // SB_HEAD_DIM)[None, :])
        h = _mlp(h, mlp_norm_g[l][None, :], w_up[l].astype(BF16), w_down[l].astype(BF16),
                 attn=attn.reshape(b * s, d), w_o=b_w_o[j].astype(BF16))
    return h.reshape(b, s, d)
```

```python
import functools
import math

import jax
import jax.numpy as jnp
from jax import lax
from jax.experimental import pallas as pl
from jax.experimental.pallas import tpu as pltpu

F32 = jnp.float32
BF16 = jnp.bfloat16
EPS = 1e-5

SSM_HEAD_DIM = 64
SSM_GROUPS = 8
D_STATE = 128
D_CONV = 4
SB_HEAD_DIM = 64

LANES = 128
SUBLANES = 8
VMEM_LIMIT_BYTES = 56 * 1024 * 1024

ROW_TILE = 512
SSD_CHUNK = 256
CONV_COL_TILE = 512
FF_TILE = 1024
ATT_TILE = 256
ATT_HEADS = 4
ATT_DEAD_LOG2DECAY = 152.0
LOG2_E = 1.4426950408889634


def _dot(a, b):
    return jnp.dot(a, b, preferred_element_type=F32)


def _dot_nt(a, b):
    return lax.dot_general(a, b, (((1,), (1,)), ((), ())), preferred_element_type=F32)


def _softplus(x):
    return jnp.maximum(x, 0.0) + jnp.log1p(jnp.exp(-jnp.abs(x)))


def _rms_scale(x):
    return lax.rsqrt(jnp.mean(x * x, axis=-1, keepdims=True) + EPS)


def _resident(shape):
    zeros = (0,) * len(shape)
    return pl.BlockSpec(shape, lambda *_: zeros, pipeline_mode=pl.Buffered(1))


def _params(*semantics):
    return pltpu.CompilerParams(dimension_semantics=semantics, vmem_limit_bytes=VMEM_LIMIT_BYTES)


def _inproj_kernel(x_ref, g_ref, w_ref, wdt_ref, zx_ref, dt_ref, *, col_tile):
    x = x_ref[...]
    u = (x * _rms_scale(x) * g_ref[...]).astype(BF16)
    for c in range(w_ref.shape[1] // col_tile):
        sl = slice(c * col_tile, (c + 1) * col_tile)
        zx_ref[:, sl] = _dot(u, w_ref[:, sl]).astype(BF16)
    dt_ref[...] = _dot(u, wdt_ref[...])


def _inproj(h, g, w_main, w_dt):
    t, d = h.shape
    n = w_main.shape[1]
    return pl.pallas_call(
        functools.partial(_inproj_kernel, col_tile=1024),
        out_shape=(jax.ShapeDtypeStruct((t, n), BF16), jax.ShapeDtypeStruct((t, LANES), F32)),
        grid=(t // ROW_TILE,),
        in_specs=[pl.BlockSpec((ROW_TILE, d), lambda i: (i, 0)),
                  _resident((1, d)), _resident((d, n)), _resident((d, LANES))],
        out_specs=(pl.BlockSpec((ROW_TILE, n), lambda i: (i, 0)),
                   pl.BlockSpec((ROW_TILE, LANES), lambda i: (i, 0))),
        compiler_params=_params("parallel"),
        name="in_proj",
    )(h, g, w_main, w_dt)


def _ssd_kernel(h_ref, z_ref, x_ref, bc_ref, dt_ref, convw_ref, convb_ref, dtb_ref, alog_ref,
                dskip_ref, gn_ref, wout_ref, out_ref, cbuf, act, state, ybuf):
    L = SSD_CHUNK
    d_inner = x_ref.shape[1]
    n_heads = d_inner // SSM_HEAD_DIM
    heads_per_group = n_heads // SSM_GROUPS
    gw = heads_per_group * SSM_HEAD_DIM
    conv_dim = cbuf.shape[1]

    @pl.when(pl.program_id(1) == 0)
    def _():
        cbuf[0:SUBLANES, :] = jnp.zeros((SUBLANES, conv_dim), F32)
        state[...] = jnp.zeros(state.shape, F32)

    cbuf[SUBLANES:SUBLANES + L, 0:d_inner] = x_ref[...].astype(F32)
    cbuf[SUBLANES:SUBLANES + L, d_inner:conv_dim] = bc_ref[...].astype(F32)
    for c in range(conv_dim // CONV_COL_TILE):
        sl = slice(c * CONV_COL_TILE, (c + 1) * CONV_COL_TILE)
        acc = convb_ref[:, sl] + convw_ref[D_CONV - 1:D_CONV, sl] * cbuf[SUBLANES:SUBLANES + L, sl]
        for k in range(D_CONV - 1):
            r0 = SUBLANES - (D_CONV - 1) + k
            acc = acc + convw_ref[k:k + 1, sl] * cbuf[r0:r0 + L, sl]
        act[:, sl] = acc * jax.nn.sigmoid(acc)
    cbuf[0:SUBLANES, :] = cbuf[L:L + SUBLANES, :]

    dt = _softplus(dt_ref[...] + dtb_ref[...])
    a = dt * (-jnp.exp(alog_ref[...]))
    row = lax.broadcasted_iota(jnp.int32, (L, L), 0)
    col = lax.broadcasted_iota(jnp.int32, (L, L), 1)
    lower = col <= row
    tri = jnp.where(lower, 1.0, 0.0).astype(BF16)
    a_hi = a.astype(BF16)
    r1 = a - a_hi.astype(F32)
    a_mid = r1.astype(BF16)
    a_lo = (r1 - a_mid.astype(F32)).astype(BF16)
    cum = _dot(tri, a_hi) + _dot(tri, a_mid) + _dot(tri, a_lo)
    cum_t = cum.T
    ecum = jnp.exp(cum)
    to_end = jnp.exp(cum[L - 1:L, :] - cum)

    for g in range(SSM_GROUPS):
        b_g = act[:, d_inner + g * D_STATE:d_inner + (g + 1) * D_STATE]
        c_off = d_inner + SSM_GROUPS * D_STATE
        c_g = act[:, c_off + g * D_STATE:c_off + (g + 1) * D_STATE].astype(BF16)
        cb = _dot_nt(c_g, b_g.astype(BF16))
        st = state[g]
        inter = _dot(c_g, st.astype(BF16))
        ys, xws, sdec = [], [], []
        for j in range(heads_per_group):
            hd = g * heads_per_group + j
            hs = slice(hd * SSM_HEAD_DIM, (hd + 1) * SSM_HEAD_DIM)
            xh = act[:, hs]
            xdt = xh * dt[:, hd:hd + 1]
            seg = cum[:, hd:hd + 1] - cum_t[hd:hd + 1, :]
            decay = jnp.exp(jnp.where(lower, seg, -jnp.inf))
            y = _dot((cb * decay).astype(BF16), xdt.astype(BF16))
            y = y + inter[:, j * SSM_HEAD_DIM:(j + 1) * SSM_HEAD_DIM] * ecum[:, hd:hd + 1]
            ys.append(y + dskip_ref[:, hs] * xh)
            xws.append((xdt * to_end[:, hd:hd + 1]).astype(BF16))
            sdec.append(jnp.broadcast_to(ecum[L - 1:L, hd:hd + 1], (1, SSM_HEAD_DIM)))
        y_g = jnp.concatenate(ys, axis=1)
        xw_g = jnp.concatenate(xws, axis=1)
        state[g] = st * jnp.concatenate(sdec, axis=1) + _dot(b_g.T.astype(BF16), xw_g)
        gs = slice(g * gw, (g + 1) * gw)
        zg = z_ref[:, gs].astype(F32)
        yg = y_g * (zg * jax.nn.sigmoid(zg))
        ybuf[:, gs] = (yg * _rms_scale(yg) * gn_ref[:, gs]).astype(BF16)

    out_ref[...] = h_ref[...] + _dot(ybuf[...], wout_ref[...])


def _ssd(h, zx, dt_raw, conv_w, conv_b, dt_bias, a_log, d_skip, gnorm_g, w_out, batch):
    t, d = h.shape
    d_inner = w_out.shape[0]
    conv_dim = conv_w.shape[1]
    assert conv_dim == 2 * d_inner, "x and B|C column blocks are addressed with one block width"
    L = SSD_CHUNK
    chunks = t // batch // L
    n_heads = d_inner // SSM_HEAD_DIM
    gw = d_inner // SSM_GROUPS
    rows = lambda b, c: (b * chunks + c, 0)
    return pl.pallas_call(
        _ssd_kernel,
        out_shape=jax.ShapeDtypeStruct((t, d), F32),
        grid=(batch, chunks),
        in_specs=[pl.BlockSpec((L, d), rows),
                  pl.BlockSpec((L, d_inner), lambda b, c: (b * chunks + c, 0)),
                  pl.BlockSpec((L, d_inner), lambda b, c: (b * chunks + c, 1)),
                  pl.BlockSpec((L, d_inner), lambda b, c: (b * chunks + c, 2)),
                  pl.BlockSpec((L, LANES), rows),
                  _resident((D_CONV, conv_dim)), _resident((1, conv_dim)),
                  _resident((1, LANES)), _resident((1, LANES)),
                  _resident((1, d_inner)), _resident((1, d_inner)), _resident((d_inner, d))],
        out_specs=pl.BlockSpec((L, d), rows),
        scratch_shapes=[pltpu.VMEM((L + SUBLANES, conv_dim), F32),
                        pltpu.VMEM((L, conv_dim), F32),
                        pltpu.VMEM((SSM_GROUPS, D_STATE, gw), F32),
                        pltpu.VMEM((L, d_inner), BF16)],
        compiler_params=_params("arbitrary", "arbitrary"),
        name="ssd",
    )(h, zx, zx, zx, dt_raw, conv_w, conv_b, dt_bias, a_log, d_skip, gnorm_g, w_out)


def _mlp_body(h, g_ref, wup_ref, wdown_ref, out_ref):
    u = (h * _rms_scale(h) * g_ref[...]).astype(BF16)
    acc = h
    for c in range(wup_ref.shape[1] // FF_TILE):
        sl = slice(c * FF_TILE, (c + 1) * FF_TILE)
        a = jnp.maximum(_dot(u, wup_ref[:, sl]), 0.0)
        acc = acc + _dot((a * a).astype(BF16), wdown_ref[sl, :])
    out_ref[...] = acc


def _mlp_kernel(h_ref, g_ref, wup_ref, wdown_ref, out_ref):
    _mlp_body(h_ref[...], g_ref, wup_ref, wdown_ref, out_ref)


def _proj_mlp_kernel(h_ref, a_ref, wo_ref, g_ref, wup_ref, wdown_ref, out_ref):
    _mlp_body(h_ref[...] + _dot(a_ref[...], wo_ref[...]), g_ref, wup_ref, wdown_ref, out_ref)


def _mlp(h, g, w_up, w_down, attn=None, w_o=None):
    t, d = h.shape
    ff = w_up.shape[1]
    rows = pl.BlockSpec((ROW_TILE, d), lambda i: (i, 0))
    weights = [_resident((1, d)), _resident((d, ff)), _resident((ff, d))]
    if attn is None:
        body, ins, specs = _mlp_kernel, (h, g, w_up, w_down), [rows] + weights
    else:
        body, ins = _proj_mlp_kernel, (h, attn, w_o, g, w_up, w_down)
        specs = [rows, rows, _resident(w_o.shape)] + weights
    return pl.pallas_call(
        body,
        out_shape=jax.ShapeDtypeStruct((t, d), F32),
        grid=(t // ROW_TILE,),
        in_specs=specs,
        out_specs=rows,
        compiler_params=_params("parallel"),
        name="mlp",
    )(*ins)


def _kvq_kernel(h_ref, gkv_ref, gq_ref, wkv_ref, wq_ref, gk_ref, q_ref, kt_ref, v_ref):
    x = h_ref[0]
    d = x.shape[1]
    xn = x * _rms_scale(x)
    ukv = (xn * gkv_ref[...]).astype(BF16)
    uq = (xn * gq_ref[...]).astype(BF16)
    q_ref[0] = _dot(uq, wq_ref[...]).astype(BF16)
    v_ref[0] = _dot(ukv, wkv_ref[:, d:2 * d]).astype(BF16)
    k_t = _dot(ukv, wkv_ref[:, 0:d]).T
    for hd in range(d // SB_HEAD_DIM):
        hs = slice(hd * SB_HEAD_DIM, (hd + 1) * SB_HEAD_DIM)
        kh = k_t[hs, :]
        scale = lax.rsqrt(jnp.mean(kh * kh, axis=0, keepdims=True) + EPS)
        kt_ref[0, hs, :] = (kh * scale * gk_ref[...]).astype(BF16)


def _kvq(h3, g_kv, g_q, w_kv, w_q, g_k):
    b, s, d = h3.shape
    rows = pl.BlockSpec((1, ROW_TILE, d), lambda bi, i: (bi, i, 0))
    return pl.pallas_call(
        _kvq_kernel,
        out_shape=(jax.ShapeDtypeStruct((b, s, d), BF16),
                   jax.ShapeDtypeStruct((b, d, s), BF16),
                   jax.ShapeDtypeStruct((b, s, d), BF16)),
        grid=(b, s // ROW_TILE),
        in_specs=[rows, _resident((1, d)), _resident((1, d)), _resident((d, 2 * d)),
                  _resident((d, d)), _resident((SB_HEAD_DIM, 1))],
        out_specs=(rows, pl.BlockSpec((1, d, ROW_TILE), lambda bi, i: (bi, 0, i)), rows),
        compiler_params=_params("parallel", "parallel"),
        name="kvq",
    )(h3, g_kv, g_q, w_kv, w_q, g_k)


def _attn_kernel(q_ref, kt_ref, v_ref, gq_ref, suffix_ref, o_ref, acc_ref, later_ref):
    T = ATT_TILE
    width = q_ref.shape[2]
    heads = width // SB_HEAD_DIM
    qi = pl.program_id(2)
    lane = lax.broadcasted_iota(jnp.int32, (T, width), 1)
    own = [jnp.logical_and(lane >= hd * SB_HEAD_DIM, lane < (hd + 1) * SB_HEAD_DIM) for hd in range(heads)]
    q = q_ref[0].astype(F32)
    sq = q * q
    inv = jnp.zeros_like(q)
    for hd in range(heads):
        ms = jnp.sum(jnp.where(own[hd], sq, 0.0), axis=-1, keepdims=True) / SB_HEAD_DIM
        inv = jnp.where(own[hd], lax.rsqrt(ms + EPS), inv)
    qn = q * inv * gq_ref[...] * (LOG2_E / math.sqrt(SB_HEAD_DIM))
    qm = [jnp.where(own[hd], qn, 0.0).astype(BF16) for hd in range(heads)]
    suffix = suffix_ref[...]
    sign_bit = jnp.uint32(0x80000000)

    def tile(kj, diagonal):
        start = pl.multiple_of(kj * T, T)
        kt = kt_ref[0, :, pl.ds(start, T)]
        if diagonal:
            row = lax.broadcasted_iota(jnp.int32, (T, T), 0)
            col = lax.broadcasted_iota(jnp.int32, (T, T), 1)
            causal = col < row
        live = None
        for hd in range(heads):
            z = _dot(qm[hd], kt)
            neg_abs = pltpu.bitcast(pltpu.bitcast(z, jnp.uint32) | sign_bit, F32)
            sp = jnp.maximum(z, 0.0) + jnp.log2(1.0 + jnp.exp2(neg_abs))
            if diagonal:
                sp = jnp.where(causal, sp, 0.0)
            cum = _dot(sp.astype(BF16), suffix)
            p = jnp.exp2(z - cum)
            if diagonal:
                p = jnp.where(causal, p, 0.0)
            half = (hd * SB_HEAD_DIM) // LANES * LANES
            pv = _dot(p.astype(BF16), v_ref[0, pl.ds(start, T), half:half + LANES])
            if diagonal:
                acc_ref[hd] = pv
                later = cum[:, 0:1]
            else:
                later = later_ref[hd]
                acc_ref[hd] += pv * jnp.exp2(-later)
                later = later + cum[:, 0:1]
            later_ref[hd] = later
            head_live = jnp.min(later)
            live = head_live if live is None else jnp.minimum(live, head_live)
        return live

    def cond(carry):
        kj, live = carry
        return jnp.logical_and(kj >= 0, live < ATT_DEAD_LOG2DECAY)

    def body(carry):
        kj, _ = carry
        return kj - 1, tile(kj, False)

    lax.while_loop(cond, body, (qi - 1, tile(qi, True)))
    per_half = LANES // SB_HEAD_DIM
    half_lane = lax.broadcasted_iota(jnp.int32, (T, LANES), 1)
    for hf in range(width // LANES):
        out = acc_ref[hf * per_half]
        for j in range(1, per_half):
            out = jnp.where(half_lane >= j * SB_HEAD_DIM, acc_ref[hf * per_half + j], out)
        o_ref[0, :, hf * LANES:(hf + 1) * LANES] = out.astype(BF16)


def _attention(q, k_t, v, g_q):
    b, s, d = q.shape
    T = ATT_TILE
    width = ATT_HEADS * SB_HEAD_DIM
    idx = jnp.arange(T)
    suffix = (idx[:, None] >= idx[None, :]).astype(BF16)
    return pl.pallas_call(
        _attn_kernel,
        out_shape=jax.ShapeDtypeStruct((b, s, d), BF16),
        grid=(b, d // width, s // T),
        in_specs=[pl.BlockSpec((1, T, width), lambda bi, hb, i: (bi, i, hb)),
                  pl.BlockSpec((1, width, s), lambda bi, hb, i: (bi, hb, 0)),
                  pl.BlockSpec((1, s, width), lambda bi, hb, i: (bi, 0, hb)),
                  _resident((1, width)), _resident((T, T))],
        out_specs=pl.BlockSpec((1, T, width), lambda bi, hb, i: (bi, i, hb)),
        scratch_shapes=[pltpu.VMEM((ATT_HEADS, T, LANES), F32),
                        pltpu.VMEM((ATT_HEADS, T, 1), F32)],
        compiler_params=_params("parallel", "parallel", "arbitrary"),
        name="attn",
    )(q, k_t, v, g_q, suffix)


def kernel(x, a_norm_g, a_w_in, a_conv_w, a_conv_b, a_dt_bias, a_a_log, a_d_skip, a_gnorm_g, a_w_out,
           kv_norm_g, w_kv, k_norm_g, b_norm_g, b_w_q, b_q_norm_g, b_w_o, mlp_norm_g, w_up, w_down):
    b, s, d = x.shape
    n_a, n_b = a_w_in.shape[0], b_w_q.shape[0]
    d_inner = a_w_out.shape[1]
    conv_dim = a_conv_w.shape[2]
    n_heads = a_dt_bias.shape[1]
    main = d_inner + conv_dim

    def lane_pad(v):
        return jnp.pad(v, (0, LANES - v.shape[0]))[None, :]

    h = x.reshape(b * s, d)
    for l in range(n_a):
        w_in = a_w_in[l]
        w_dt = jnp.pad(w_in[:, main:], ((0, 0), (0, LANES - n_heads))).astype(BF16)
        zx, dt_raw = _inproj(h, a_norm_g[l][None, :], w_in[:, :main].astype(BF16), w_dt)
        h = _ssd(h, zx, dt_raw, a_conv_w[l], a_conv_b[l][None, :], lane_pad(a_dt_bias[l]),
                 lane_pad(a_a_log[l]), jnp.repeat(a_d_skip[l], SSM_HEAD_DIM)[None, :],
                 a_gnorm_g[l][None, :], a_w_out[l].astype(BF16), b)
        h = _mlp(h, mlp_norm_g[l][None, :], w_up[l].astype(BF16), w_down[l].astype(BF16))
    q = k_t = v = None
    for j in range(n_b):
        l = n_a + j
        if j == 0:
            q, k_t, v = _kvq(h.reshape(b, s, d), kv_norm_g[None, :], b_norm_g[j][None, :],
                             w_kv.astype(BF16), b_w_q[j].astype(BF16), k_norm_g[:, None])
        else:
            raise NotImplementedError("one stick-breaking layer per shared K/V projection call")
        attn = _attention(q, k_t, v, jnp.tile(b_q_norm_g[j], ATT_HEADS)[None, :])
        h = _mlp(h, mlp_norm_g[l][None, :], w_up[l].astype(BF16), w_down[l].astype(BF16),
                 attn=attn.reshape(b * s, d), w_o=b_w_o[j].astype(BF16))
    return h.reshape(b, s, d)
```

```python
import functools
import math

import jax
import jax.numpy as jnp
from jax import lax
from jax.experimental import pallas as pl
from jax.experimental.pallas import tpu as pltpu

F32 = jnp.float32
BF16 = jnp.bfloat16
EPS = 1e-5
LOG2_E = 1.4426950408889634

SSM_HEAD_DIM = 64
SSM_GROUPS = 8
D_STATE = 128
D_CONV = 4
SB_HEAD_DIM = 64

LANES = 128
SUBLANES = 8
VMEM_LIMIT_BYTES = 56 * 1024 * 1024

ROW_TILE = 512
SSD_CHUNK = 256
FF_TILE = 1024
ATT_TILE = 256
ATT_HEADS = 4
ATT_DEAD_LOG2DECAY = 152.0


def _dot(a, b):
    return jnp.dot(a, b, preferred_element_type=F32)


def _dot_nt(a, b):
    return lax.dot_general(a, b, (((1,), (1,)), ((), ())), preferred_element_type=F32)


def _softplus(x):
    return jnp.maximum(x, 0.0) + jnp.log1p(jnp.exp(-jnp.abs(x)))


def _silu(x):
    return x * jax.nn.sigmoid(x)


def _rms_scale(x):
    return lax.rsqrt(jnp.mean(x * x, axis=-1, keepdims=True) + EPS)


def _resident(shape):
    zeros = (0,) * len(shape)
    return pl.BlockSpec(shape, lambda *_: zeros, pipeline_mode=pl.Buffered(1))


def _params(*semantics):
    return pltpu.CompilerParams(dimension_semantics=semantics, vmem_limit_bytes=VMEM_LIMIT_BYTES)


def _inproj_kernel(x_ref, g_ref, w_ref, wdt_ref, convw_ref, convb_ref, z_ref, act_ref, dt_ref, edge,
                   *, steps_per_seq, col_tile):
    tm = x_ref.shape[0]
    d_inner = z_ref.shape[1]
    conv_dim = act_ref.shape[1]
    tail = D_CONV - 1

    @pl.when(pl.program_id(0) % steps_per_seq == 0)
    def _():
        edge[0:SUBLANES, :] = jnp.zeros((SUBLANES, conv_dim), F32)

    x = x_ref[...]
    u = (x * _rms_scale(x) * g_ref[...]).astype(BF16)
    dt_ref[...] = _dot(u, wdt_ref[...])
    n_conv = conv_dim // col_tile
    z_tile = d_inner // n_conv
    for c in range(n_conv):
        zs = slice(c * z_tile, (c + 1) * z_tile)
        z_ref[:, zs] = _dot(u, w_ref[:, zs]).astype(BF16)
        sl = slice(c * col_tile, (c + 1) * col_tile)
        cur = _dot(u, w_ref[:, d_inner + c * col_tile:d_inner + (c + 1) * col_tile])
        w = [convw_ref[k:k + 1, sl] for k in range(D_CONV)]
        prev = pltpu.roll(cur, 1, axis=0)
        acc = pltpu.roll(w[0] * prev + w[1] * cur, 2, axis=0) + (w[2] * prev + w[3] * cur) + convb_ref[:, sl]
        edge[SUBLANES:2 * SUBLANES, sl] = cur[0:SUBLANES]
        head = convb_ref[:, sl] + w[tail] * cur[0:SUBLANES]
        for k in range(tail):
            r0 = SUBLANES - tail + k
            head = head + w[k] * edge[r0:r0 + SUBLANES, sl]
        edge[0:SUBLANES, sl] = cur[tm - SUBLANES:tm]
        act_ref[:, sl] = _silu(jnp.concatenate([head, acc[SUBLANES:tm]], axis=0)).astype(BF16)


def _inproj(h, g, w_main, w_dt, conv_w, conv_b, seq_len):
    t, d = h.shape
    conv_dim = conv_w.shape[1]
    d_inner = w_main.shape[1] - conv_dim
    rows = lambda width: pl.BlockSpec((ROW_TILE, width), lambda i: (i, 0))
    return pl.pallas_call(
        functools.partial(_inproj_kernel, steps_per_seq=seq_len // ROW_TILE, col_tile=512),
        out_shape=(jax.ShapeDtypeStruct((t, d_inner), BF16),
                   jax.ShapeDtypeStruct((t, conv_dim), BF16),
                   jax.ShapeDtypeStruct((t, LANES), F32)),
        grid=(t // ROW_TILE,),
        in_specs=[rows(d), _resident((1, d)), _resident(w_main.shape), _resident((d, LANES)),
                  _resident((D_CONV, conv_dim)), _resident((1, conv_dim))],
        out_specs=(rows(d_inner), rows(conv_dim), rows(LANES)),
        scratch_shapes=[pltpu.VMEM((2 * SUBLANES, conv_dim), F32)],
        compiler_params=_params("arbitrary"),
        name="in_proj",
    )(h, g, w_main, w_dt, conv_w, conv_b)


def _ssd_kernel(h_ref, z_ref, x_ref, bc_ref, dt_ref, dtb_ref, alog_ref, dskip_ref, gn_ref, expand_ref,
                wout_ref, out_ref, state, ybuf, ecum_x, wgt_x):
    L = SSD_CHUNK
    H = L // 2
    d_inner = x_ref.shape[1]
    n_heads = d_inner // SSM_HEAD_DIM
    heads_per_group = n_heads // SSM_GROUPS
    gw = heads_per_group * SSM_HEAD_DIM
    n_state = SSM_GROUPS * D_STATE

    @pl.when(pl.program_id(1) == 0)
    def _():
        state[...] = jnp.zeros(state.shape, F32)

    dt = _softplus(dt_ref[...] + dtb_ref[...])
    a = dt * (-LOG2_E * jnp.exp(alog_ref[...]))
    row = lax.broadcasted_iota(jnp.int32, (L, L), 0)
    col = lax.broadcasted_iota(jnp.int32, (L, L), 1)
    tri = jnp.where(col <= row, 1.0, 0.0).astype(BF16)
    a_hi = a.astype(BF16)
    r1 = a - a_hi.astype(F32)
    a_mid = r1.astype(BF16)
    a_lo = (r1 - a_mid.astype(F32)).astype(BF16)
    parts = _dot(tri, jnp.concatenate([a_hi, a_mid, a_lo], axis=1))
    cum = parts[:, 0:LANES] + parts[:, LANES:2 * LANES] + parts[:, 2 * LANES:3 * LANES]
    key_t = (cum - jnp.log2(dt)).T
    ecum_x[...] = _dot(jnp.exp2(cum).astype(BF16), expand_ref[...])
    wgt_x[...] = _dot((dt * jnp.exp2(cum[L - 1:L, :] - cum)).astype(BF16), expand_ref[...])

    hrow = lax.broadcasted_iota(jnp.int32, (H, H), 0)
    hcol = lax.broadcasted_iota(jnp.int32, (H, H), 1)
    lower = hcol <= hrow
    glane = lax.broadcasted_iota(jnp.int32, (1, gw), 1)
    head_lanes = [jnp.where(jnp.logical_and(glane >= j * SSM_HEAD_DIM, glane < (j + 1) * SSM_HEAD_DIM),
                            1.0, 0.0).astype(BF16) for j in range(heads_per_group)]

    for g in range(SSM_GROUPS):
        gs = slice(g * gw, (g + 1) * gw)
        xb = x_ref[:, gs]
        x_g = xb.astype(F32)
        b_g = bc_ref[:, g * D_STATE:(g + 1) * D_STATE]
        c_g = bc_ref[:, n_state + g * D_STATE:n_state + (g + 1) * D_STATE]
        cb = _dot_nt(c_g, b_g)
        st = state[g]
        y_top = jnp.zeros((H, gw), F32)
        y_bot = jnp.zeros((H, gw), F32)
        for j in range(heads_per_group):
            hd = g * heads_per_group + j
            xm = xb * head_lanes[j]
            q_i = cum[:, hd:hd + 1]
            k_j = key_t[hd:hd + 1, :]
            d_tl = jnp.exp2(jnp.where(lower, q_i[0:H] - k_j[:, 0:H], -jnp.inf)) * cb[0:H, 0:H]
            d_bl = jnp.exp2(q_i[H:L] - k_j[:, 0:H]) * cb[H:L, 0:H]
            d_br = jnp.exp2(jnp.where(lower, q_i[H:L] - k_j[:, H:L], -jnp.inf)) * cb[H:L, H:L]
            y_top = y_top + _dot(d_tl.astype(BF16), xm[0:H])
            y_bot = y_bot + _dot(jnp.concatenate([d_bl, d_br], axis=1).astype(BF16), xm)
        inter = _dot(c_g, st.astype(BF16)) * ecum_x[:, gs]
        y_g = jnp.concatenate([y_top, y_bot], axis=0) + inter + dskip_ref[:, gs] * x_g
        xw = (x_g * wgt_x[:, gs]).astype(BF16)
        state[g] = st * ecum_x[L - 1:L, gs] + _dot(b_g.astype(F32).T.astype(BF16), xw)
        yg = y_g * _silu(z_ref[:, gs].astype(F32))
        ybuf[:, gs] = (yg * _rms_scale(yg) * gn_ref[:, gs]).astype(BF16)

    out_ref[...] = h_ref[...] + _dot(ybuf[...], wout_ref[...])


def _ssd(h, z, act, dt_raw, dt_bias, a_log, d_skip, gnorm_g, w_out, batch):
    t, d = h.shape
    d_inner = w_out.shape[0]
    assert act.shape[1] == 2 * d_inner, "x and B|C column blocks are addressed with one block width"
    L = SSD_CHUNK
    chunks = t // batch // L
    gw = d_inner // SSM_GROUPS
    expand = (jnp.arange(LANES)[:, None] == jnp.arange(d_inner)[None, :] // SSM_HEAD_DIM).astype(BF16)
    rows = lambda b, c: (b * chunks + c, 0)
    return pl.pallas_call(
        _ssd_kernel,
        out_shape=jax.ShapeDtypeStruct((t, d), F32),
        grid=(batch, chunks),
        in_specs=[pl.BlockSpec((L, d), rows),
                  pl.BlockSpec((L, d_inner), rows),
                  pl.BlockSpec((L, d_inner), rows),
                  pl.BlockSpec((L, d_inner), lambda b, c: (b * chunks + c, 1)),
                  pl.BlockSpec((L, LANES), rows),
                  _resident((1, LANES)), _resident((1, LANES)),
                  _resident((1, d_inner)), _resident((1, d_inner)),
                  _resident((LANES, d_inner)), _resident((d_inner, d))],
        out_specs=pl.BlockSpec((L, d), rows),
        scratch_shapes=[pltpu.VMEM((SSM_GROUPS, D_STATE, gw), F32),
                        pltpu.VMEM((L, d_inner), BF16),
                        pltpu.VMEM((L, d_inner), F32),
                        pltpu.VMEM((L, d_inner), F32)],
        compiler_params=_params("arbitrary", "arbitrary"),
        name="ssd",
    )(h, z, act, act, dt_raw, dt_bias, a_log, d_skip, gnorm_g, expand, w_out)


def _mlp_body(h, g_ref, wup_ref, wdown_ref, out_ref):
    u = (h * _rms_scale(h) * g_ref[...]).astype(BF16)
    acc = h
    for c in range(wup_ref.shape[1] // FF_TILE):
        sl = slice(c * FF_TILE, (c + 1) * FF_TILE)
        a = jnp.maximum(_dot(u, wup_ref[:, sl]), 0.0)
        acc = acc + _dot((a * a).astype(BF16), wdown_ref[sl, :])
    out_ref[...] = acc


def _mlp_kernel(h_ref, g_ref, wup_ref, wdown_ref, out_ref):
    _mlp_body(h_ref[...], g_ref, wup_ref, wdown_ref, out_ref)


def _proj_mlp_kernel(h_ref, a_ref, wo_ref, g_ref, wup_ref, wdown_ref, out_ref):
    _mlp_body(h_ref[...] + _dot(a_ref[...], wo_ref[...]), g_ref, wup_ref, wdown_ref, out_ref)


def _mlp(h, g, w_up, w_down, attn=None, w_o=None):
    t, d = h.shape
    ff = w_up.shape[1]
    rows = pl.BlockSpec((ROW_TILE, d), lambda i: (i, 0))
    weights = [_resident((1, d)), _resident((d, ff)), _resident((ff, d))]
    if attn is None:
        body, ins, specs = _mlp_kernel, (h, g, w_up, w_down), [rows] + weights
    else:
        body, ins = _proj_mlp_kernel, (h, attn, w_o, g, w_up, w_down)
        specs = [rows, rows, _resident(w_o.shape)] + weights
    return pl.pallas_call(
        body,
        out_shape=jax.ShapeDtypeStruct((t, d), F32),
        grid=(t // ROW_TILE,),
        in_specs=specs,
        out_specs=rows,
        compiler_params=_params("parallel"),
        name="mlp",
    )(*ins)


def _kvq_kernel(h_ref, gkv_ref, gq_ref, wkv_ref, wq_ref, gk_ref, q_ref, kt_ref, v_ref):
    x = h_ref[0]
    d = x.shape[1]
    xn = x * _rms_scale(x)
    ukv = (xn * gkv_ref[...]).astype(BF16)
    uq = (xn * gq_ref[...]).astype(BF16)
    q_ref[0] = _dot(uq, wq_ref[...]).astype(BF16)
    v_ref[0] = _dot(ukv, wkv_ref[:, d:2 * d]).astype(BF16)
    k_t = _dot(ukv, wkv_ref[:, 0:d]).T
    for hd in range(d // SB_HEAD_DIM):
        hs = slice(hd * SB_HEAD_DIM, (hd + 1) * SB_HEAD_DIM)
        kh = k_t[hs, :]
        scale = lax.rsqrt(jnp.mean(kh * kh, axis=0, keepdims=True) + EPS)
        kt_ref[0, hs, :] = (kh * scale * gk_ref[...]).astype(BF16)


def _kvq(h3, g_kv, g_q, w_kv, w_q, g_k):
    b, s, d = h3.shape
    rows = pl.BlockSpec((1, ROW_TILE, d), lambda bi, i: (bi, i, 0))
    return pl.pallas_call(
        _kvq_kernel,
        out_shape=(jax.ShapeDtypeStruct((b, s, d), BF16),
                   jax.ShapeDtypeStruct((b, d, s), BF16),
                   jax.ShapeDtypeStruct((b, s, d), BF16)),
        grid=(b, s // ROW_TILE),
        in_specs=[rows, _resident((1, d)), _resident((1, d)), _resident((d, 2 * d)),
                  _resident((d, d)), _resident((SB_HEAD_DIM, 1))],
        out_specs=(rows, pl.BlockSpec((1, d, ROW_TILE), lambda bi, i: (bi, 0, i)), rows),
        compiler_params=_params("parallel", "parallel"),
        name="kvq",
    )(h3, g_kv, g_q, w_kv, w_q, g_k)


def _attn_kernel(q_ref, kt_ref, v_ref, gq_ref, suffix_ref, o_ref, acc_ref, later_ref):
    T = ATT_TILE
    width = q_ref.shape[2]
    heads = width // SB_HEAD_DIM
    qi = pl.program_id(2)
    lane = lax.broadcasted_iota(jnp.int32, (T, width), 1)
    own = [jnp.logical_and(lane >= hd * SB_HEAD_DIM, lane < (hd + 1) * SB_HEAD_DIM) for hd in range(heads)]
    q = q_ref[0].astype(F32)
    sq = q * q
    inv = jnp.zeros_like(q)
    for hd in range(heads):
        ms = jnp.sum(jnp.where(own[hd], sq, 0.0), axis=-1, keepdims=True) / SB_HEAD_DIM
        inv = jnp.where(own[hd], lax.rsqrt(ms + EPS), inv)
    qn = q * inv * gq_ref[...] * (LOG2_E / math.sqrt(SB_HEAD_DIM))
    qm = [jnp.where(own[hd], qn, 0.0).astype(BF16) for hd in range(heads)]
    suffix = suffix_ref[...]
    sign_bit = jnp.uint32(0x80000000)

    row = lax.broadcasted_iota(jnp.int32, (T, T), 0)
    col = lax.broadcasted_iota(jnp.int32, (T, T), 1)
    causal = col < row

    def head_tile(hd, kj, diagonal):
        start = pl.multiple_of(kj * T, T)
        z = _dot(qm[hd], kt_ref[0, :, pl.ds(start, T)])
        neg_abs = pltpu.bitcast(pltpu.bitcast(z, jnp.uint32) | sign_bit, F32)
        sp = jnp.maximum(z, 0.0) + jnp.log2(1.0 + jnp.exp2(neg_abs))
        if diagonal:
            sp = jnp.where(causal, sp, 0.0)
        cum = _dot(sp.astype(BF16), suffix)
        p = jnp.exp2(z - cum)
        if diagonal:
            p = jnp.where(causal, p, 0.0)
        half = (hd * SB_HEAD_DIM) // LANES * LANES
        pv = _dot(p.astype(BF16), v_ref[0, pl.ds(start, T), half:half + LANES])
        return pv, cum[:, 0:1]

    has_prev = qi > 0
    prev = jnp.maximum(qi - 1, 0)
    live = None
    for hd in range(heads):
        pv_d, later = head_tile(hd, qi, True)
        pv_p, tot_p = head_tile(hd, prev, False)
        acc_ref[hd] = pv_d + pv_p * jnp.where(has_prev, jnp.exp2(-later), 0.0)
        later = later + jnp.where(has_prev, tot_p, 0.0)
        later_ref[hd] = later
        head_live = jnp.min(later)
        live = head_live if live is None else jnp.minimum(live, head_live)

    def cond(carry):
        kj, live = carry
        return jnp.logical_and(kj >= 0, live < ATT_DEAD_LOG2DECAY)

    def body(carry):
        kj, _ = carry
        live = None
        for hd in range(heads):
            pv, tot = head_tile(hd, kj, False)
            later = later_ref[hd]
            acc_ref[hd] += pv * jnp.exp2(-later)
            later = later + tot
            later_ref[hd] = later
            head_live = jnp.min(later)
            live = head_live if live is None else jnp.minimum(live, head_live)
        return kj - 1, live

    lax.while_loop(cond, body, (qi - 2, live))
    per_half = LANES // SB_HEAD_DIM
    half_lane = lax.broadcasted_iota(jnp.int32, (T, LANES), 1)
    for hf in range(width // LANES):
        out = acc_ref[hf * per_half]
        for j in range(1, per_half):
            out = jnp.where(half_lane >= j * SB_HEAD_DIM, acc_ref[hf * per_half + j], out)
        o_ref[0, :, hf * LANES:(hf + 1) * LANES] = out.astype(BF16)


def _attention(q, k_t, v, g_q):
    b, s, d = q.shape
    T = ATT_TILE
    width = ATT_HEADS * SB_HEAD_DIM
    idx = jnp.arange(T)
    suffix = (idx[:, None] >= idx[None, :]).astype(BF16)
    return pl.pallas_call(
        _attn_kernel,
        out_shape=jax.ShapeDtypeStruct((b, s, d), BF16),
        grid=(b, d // width, s // T),
        in_specs=[pl.BlockSpec((1, T, width), lambda bi, hb, i: (bi, i, hb)),
                  pl.BlockSpec((1, width, s), lambda bi, hb, i: (bi, hb, 0)),
                  pl.BlockSpec((1, s, width), lambda bi, hb, i: (bi, 0, hb)),
                  _resident((1, width)), _resident((T, T))],
        out_specs=pl.BlockSpec((1, T, width), lambda bi, hb, i: (bi, i, hb)),
        scratch_shapes=[pltpu.VMEM((ATT_HEADS, T, LANES), F32),
                        pltpu.VMEM((ATT_HEADS, T, 1), F32)],
        compiler_params=_params("parallel", "parallel", "arbitrary"),
        name="attn",
    )(q, k_t, v, g_q, suffix)


def kernel(x, a_norm_g, a_w_in, a_conv_w, a_conv_b, a_dt_bias, a_a_log, a_d_skip, a_gnorm_g, a_w_out,
           kv_norm_g, w_kv, k_norm_g, b_norm_g, b_w_q, b_q_norm_g, b_w_o, mlp_norm_g, w_up, w_down):
    b, s, d = x.shape
    n_a, n_b = a_w_in.shape[0], b_w_q.shape[0]
    d_inner = a_w_out.shape[1]
    conv_dim = a_conv_w.shape[2]
    n_heads = a_dt_bias.shape[1]
    main = d_inner + conv_dim

    def lane_pad(v):
        return jnp.pad(v, (0, LANES - v.shape[0]))[None, :]

    h = x.reshape(b * s, d)
    for l in range(n_a):
        w_in = a_w_in[l]
        w_dt = jnp.pad(w_in[:, main:], ((0, 0), (0, LANES - n_heads))).astype(BF16)
        z, act, dt_raw = _inproj(h, a_norm_g[l][None, :], w_in[:, :main].astype(BF16), w_dt,
                                 a_conv_w[l], a_conv_b[l][None, :], s)
        h = _ssd(h, z, act, dt_raw, lane_pad(a_dt_bias[l]), lane_pad(a_a_log[l]),
                 jnp.repeat(a_d_skip[l], SSM_HEAD_DIM)[None, :], a_gnorm_g[l][None, :],
                 a_w_out[l].astype(BF16), b)
        h = _mlp(h, mlp_norm_g[l][None, :], w_up[l].astype(BF16), w_down[l].astype(BF16))
    q = k_t = v = None
    for j in range(n_b):
        l = n_a + j
        if j == 0:
            q, k_t, v = _kvq(h.reshape(b, s, d), kv_norm_g[None, :], b_norm_g[j][None, :],
                             w_kv.astype(BF16), b_w_q[j].astype(BF16), k_norm_g[:, None])
        else:
            raise NotImplementedError("one stick-breaking layer per shared K/V projection call")
        attn = _attention(q, k_t, v, jnp.tile(b_q_norm_g[j], ATT_HEADS)[None, :])
        h = _mlp(h, mlp_norm_g[l][None, :], w_up[l].astype(BF16), w_down[l].astype(BF16),
                 attn=attn.reshape(b * s, d), w_o=b_w_o[j].astype(BF16))
    return h.reshape(b, s, d)
```

```python
import functools
import math

import jax
import jax.numpy as jnp
from jax import lax
from jax.experimental import pallas as pl
from jax.experimental.pallas import tpu as pltpu

F32 = jnp.float32
BF16 = jnp.bfloat16
EPS = 1e-5
LOG2_E = 1.4426950408889634

SSM_HEAD_DIM = 64
SSM_GROUPS = 8
D_STATE = 128
D_CONV = 4
SB_HEAD_DIM = 64

LANES = 128
SUBLANES = 8
VMEM_LIMIT_BYTES = 56 * 1024 * 1024

ROW_TILE = 512
INPROJ_ROWS = 256
SSD_CHUNK = 256
FF_TILE = 1024
ATT_ROWS = 256
ATT_KEYS = 256
ATT_HEADS = 4
ATT_Q_TILES = 2
ATT_DEAD_LOG2DECAY = 152.0
MASKED_LOGIT = -1e30


def _dot(a, b):
    return jnp.dot(a, b, preferred_element_type=F32)


def _dot_nt(a, b):
    return lax.dot_general(a, b, (((1,), (1,)), ((), ())), preferred_element_type=F32)


def _softplus(x):
    return jnp.maximum(x, 0.0) + jnp.log1p(jnp.exp(-jnp.abs(x)))


def _silu(x):
    return x * jax.nn.sigmoid(x)


def _rms_scale(x):
    return lax.rsqrt(jnp.mean(x * x, axis=-1, keepdims=True) + EPS)


def _resident(shape):
    zeros = (0,) * len(shape)
    return pl.BlockSpec(shape, lambda *_: zeros, pipeline_mode=pl.Buffered(1))


def _params(*semantics):
    return pltpu.CompilerParams(dimension_semantics=semantics, vmem_limit_bytes=VMEM_LIMIT_BYTES)


def _inproj_kernel(x_ref, g_ref, w_ref, convw_ref, convb_ref, z_ref, act_ref, dt_ref, edge,
                   *, steps_per_seq, col_tile):
    tm = x_ref.shape[0]
    d_inner = z_ref.shape[1]
    conv_dim = act_ref.shape[1]
    tail = D_CONV - 1

    @pl.when(pl.program_id(0) % steps_per_seq == 0)
    def _():
        edge[0:SUBLANES, :] = jnp.zeros((SUBLANES, conv_dim), F32)

    x = x_ref[...]
    u = (x * _rms_scale(x) * g_ref[...]).astype(BF16)
    dt_ref[...] = _dot(u, w_ref[:, d_inner + conv_dim:d_inner + conv_dim + LANES])
    n_conv = conv_dim // col_tile
    z_tile = d_inner // n_conv
    for c in range(n_conv):
        zs = slice(c * z_tile, (c + 1) * z_tile)
        z_ref[:, zs] = _dot(u, w_ref[:, zs]).astype(BF16)
        sl = slice(c * col_tile, (c + 1) * col_tile)
        cur = _dot(u, w_ref[:, d_inner + c * col_tile:d_inner + (c + 1) * col_tile])
        w = [convw_ref[k:k + 1, sl] for k in range(D_CONV)]
        prev = pltpu.roll(cur, 1, axis=0)
        acc = pltpu.roll(w[0] * prev + w[1] * cur, 2, axis=0) + (w[2] * prev + w[3] * cur) + convb_ref[:, sl]
        edge[SUBLANES:2 * SUBLANES, sl] = cur[0:SUBLANES]
        head = convb_ref[:, sl] + w[tail] * cur[0:SUBLANES]
        for k in range(tail):
            r0 = SUBLANES - tail + k
            head = head + w[k] * edge[r0:r0 + SUBLANES, sl]
        edge[0:SUBLANES, sl] = cur[tm - SUBLANES:tm]
        act_ref[:, sl] = _silu(jnp.concatenate([head, acc[SUBLANES:tm]], axis=0)).astype(BF16)


def _inproj(h, g, w_all, conv_w, conv_b, seq_len):
    t, d = h.shape
    conv_dim = conv_w.shape[1]
    d_inner = w_all.shape[1] - conv_dim - LANES
    rows = lambda width: pl.BlockSpec((INPROJ_ROWS, width), lambda i: (i, 0))
    return pl.pallas_call(
        functools.partial(_inproj_kernel, steps_per_seq=seq_len // INPROJ_ROWS, col_tile=512),
        out_shape=(jax.ShapeDtypeStruct((t, d_inner), BF16),
                   jax.ShapeDtypeStruct((t, conv_dim), BF16),
                   jax.ShapeDtypeStruct((t, LANES), F32)),
        grid=(t // INPROJ_ROWS,),
        in_specs=[rows(d), _resident((1, d)), _resident(w_all.shape),
                  _resident((D_CONV, conv_dim)), _resident((1, conv_dim))],
        out_specs=(rows(d_inner), rows(conv_dim), rows(LANES)),
        scratch_shapes=[pltpu.VMEM((2 * SUBLANES, conv_dim), F32)],
        compiler_params=_params("arbitrary"),
        name="in_proj",
    )(h, g, w_all, conv_w, conv_b)


def _ssd_kernel(h_ref, z_ref, x_ref, bc_ref, dt_ref, dtb_ref, alog_ref, dskip_ref, gn_ref, expand_ref,
                wout_ref, out_ref, state, ybuf, ecum_x, wgt_x):
    L = SSD_CHUNK
    H = L // 2
    d_inner = x_ref.shape[1]
    n_heads = d_inner // SSM_HEAD_DIM
    heads_per_group = n_heads // SSM_GROUPS
    gw = heads_per_group * SSM_HEAD_DIM
    n_state = SSM_GROUPS * D_STATE

    @pl.when(pl.program_id(1) == 0)
    def _():
        state[...] = jnp.zeros(state.shape, F32)

    dt = _softplus(dt_ref[...] + dtb_ref[...])
    a = dt * (-LOG2_E * jnp.exp(alog_ref[...]))
    row = lax.broadcasted_iota(jnp.int32, (L, L), 0)
    col = lax.broadcasted_iota(jnp.int32, (L, L), 1)
    tri = jnp.where(col <= row, 1.0, 0.0).astype(BF16)
    a_hi = a.astype(BF16)
    r1 = a - a_hi.astype(F32)
    a_mid = r1.astype(BF16)
    a_lo = (r1 - a_mid.astype(F32)).astype(BF16)
    parts = _dot(tri, jnp.concatenate([a_hi, a_mid, a_lo], axis=1))
    cum = parts[:, 0:LANES] + parts[:, LANES:2 * LANES] + parts[:, 2 * LANES:3 * LANES]
    key_t = (cum - jnp.log2(dt)).T
    ecum_x[...] = _dot(jnp.exp2(cum).astype(BF16), expand_ref[...])
    wgt_x[...] = _dot((dt * jnp.exp2(cum[L - 1:L, :] - cum)).astype(BF16), expand_ref[...])

    hrow = lax.broadcasted_iota(jnp.int32, (H, H), 0)
    hcol = lax.broadcasted_iota(jnp.int32, (H, H), 1)
    lower = hcol <= hrow
    glane = lax.broadcasted_iota(jnp.int32, (1, gw), 1)
    head_lanes = [jnp.where(jnp.logical_and(glane >= j * SSM_HEAD_DIM, glane < (j + 1) * SSM_HEAD_DIM),
                            1.0, 0.0).astype(BF16) for j in range(heads_per_group)]

    for g in range(SSM_GROUPS):
        gs = slice(g * gw, (g + 1) * gw)
        xb = x_ref[:, gs]
        x_g = xb.astype(F32)
        b_g = bc_ref[:, g * D_STATE:(g + 1) * D_STATE]
        c_g = bc_ref[:, n_state + g * D_STATE:n_state + (g + 1) * D_STATE]
        cb = _dot_nt(c_g, b_g)
        st = state[g]
        y_top = jnp.zeros((H, gw), F32)
        y_bot = jnp.zeros((H, gw), F32)
        for j in range(heads_per_group):
            hd = g * heads_per_group + j
            xm = xb * head_lanes[j]
            q_i = cum[:, hd:hd + 1]
            k_j = key_t[hd:hd + 1, :]
            d_tl = jnp.exp2(jnp.where(lower, q_i[0:H] - k_j[:, 0:H], -jnp.inf)) * cb[0:H, 0:H]
            d_bl = jnp.exp2(q_i[H:L] - k_j[:, 0:H]) * cb[H:L, 0:H]
            d_br = jnp.exp2(jnp.where(lower, q_i[H:L] - k_j[:, H:L], -jnp.inf)) * cb[H:L, H:L]
            y_top = y_top + _dot(d_tl.astype(BF16), xm[0:H])
            y_bot = y_bot + _dot(jnp.concatenate([d_bl, d_br], axis=1).astype(BF16), xm)
        inter = _dot(c_g, st.astype(BF16)) * ecum_x[:, gs]
        y_g = jnp.concatenate([y_top, y_bot], axis=0) + inter + dskip_ref[:, gs] * x_g
        xw = (x_g * wgt_x[:, gs]).astype(BF16)
        state[g] = st * ecum_x[L - 1:L, gs] + _dot(b_g.astype(F32).T.astype(BF16), xw)
        yg = y_g * _silu(z_ref[:, gs].astype(F32))
        ybuf[:, gs] = (yg * _rms_scale(yg) * gn_ref[:, gs]).astype(BF16)

    out_ref[...] = h_ref[...] + _dot(ybuf[...], wout_ref[...])


def _ssd(h, z, act, dt_raw, dt_bias, a_log, d_skip, gnorm_g, w_out, batch):
    t, d = h.shape
    d_inner = w_out.shape[0]
    assert act.shape[1] == 2 * d_inner, "x and B|C column blocks are addressed with one block width"
    L = SSD_CHUNK
    chunks = t // batch // L
    gw = d_inner // SSM_GROUPS
    expand = (jnp.arange(LANES)[:, None] == jnp.arange(d_inner)[None, :] // SSM_HEAD_DIM).astype(BF16)
    rows = lambda b, c: (b * chunks + c, 0)
    return pl.pallas_call(
        _ssd_kernel,
        out_shape=jax.ShapeDtypeStruct((t, d), F32),
        grid=(batch, chunks),
        in_specs=[pl.BlockSpec((L, d), rows),
                  pl.BlockSpec((L, d_inner), rows),
                  pl.BlockSpec((L, d_inner), rows),
                  pl.BlockSpec((L, d_inner), lambda b, c: (b * chunks + c, 1)),
                  pl.BlockSpec((L, LANES), rows),
                  _resident((1, LANES)), _resident((1, LANES)),
                  _resident((1, d_inner)), _resident((1, d_inner)),
                  _resident((LANES, d_inner)), _resident((d_inner, d))],
        out_specs=pl.BlockSpec((L, d), rows),
        scratch_shapes=[pltpu.VMEM((SSM_GROUPS, D_STATE, gw), F32),
                        pltpu.VMEM((L, d_inner), BF16),
                        pltpu.VMEM((L, d_inner), F32),
                        pltpu.VMEM((L, d_inner), F32)],
        compiler_params=_params("arbitrary", "arbitrary"),
        name="ssd",
    )(h, z, act, act, dt_raw, dt_bias, a_log, d_skip, gnorm_g, expand, w_out)


def _mlp_body(h, g_ref, wup_ref, wdown_ref, out_ref):
    u = (h * _rms_scale(h) * g_ref[...]).astype(BF16)
    acc = h
    for c in range(wup_ref.shape[1] // FF_TILE):
        sl = slice(c * FF_TILE, (c + 1) * FF_TILE)
        a = jnp.maximum(_dot(u, wup_ref[:, sl]), 0.0)
        acc = acc + _dot((a * a).astype(BF16), wdown_ref[sl, :])
    out_ref[...] = acc


def _mlp_kernel(h_ref, g_ref, wup_ref, wdown_ref, out_ref):
    _mlp_body(h_ref[...], g_ref, wup_ref, wdown_ref, out_ref)


def _proj_mlp_kernel(h_ref, a_ref, wo_ref, g_ref, wup_ref, wdown_ref, out_ref):
    _mlp_body(h_ref[...] + _dot(a_ref[...], wo_ref[...]), g_ref, wup_ref, wdown_ref, out_ref)


def _mlp(h, g, w_up, w_down, attn=None, w_o=None):
    t, d = h.shape
    ff = w_up.shape[1]
    rows = pl.BlockSpec((ROW_TILE, d), lambda i: (i, 0))
    weights = [_resident((1, d)), _resident((d, ff)), _resident((ff, d))]
    if attn is None:
        body, ins, specs = _mlp_kernel, (h, g, w_up, w_down), [rows] + weights
    else:
        body, ins = _proj_mlp_kernel, (h, attn, w_o, g, w_up, w_down)
        specs = [rows, rows, _resident(w_o.shape)] + weights
    return pl.pallas_call(
        body,
        out_shape=jax.ShapeDtypeStruct((t, d), F32),
        grid=(t // ROW_TILE,),
        in_specs=specs,
        out_specs=rows,
        compiler_params=_params("parallel"),
        name="mlp",
    )(*ins)


def _kvq_kernel(h_ref, gkv_ref, gq_ref, wkv_ref, wq_ref, gk_ref, q_ref, kt_ref, v_ref):
    x = h_ref[0]
    d = x.shape[1]
    xn = x * _rms_scale(x)
    ukv = (xn * gkv_ref[...]).astype(BF16)
    uq = (xn * gq_ref[...]).astype(BF16)
    q_ref[0] = _dot(uq, wq_ref[...]).astype(BF16)
    v_ref[0] = _dot(ukv, wkv_ref[:, d:2 * d]).astype(BF16)
    k_t = _dot(ukv, wkv_ref[:, 0:d]).T
    for hd in range(d // SB_HEAD_DIM):
        hs = slice(hd * SB_HEAD_DIM, (hd + 1) * SB_HEAD_DIM)
        kh = k_t[hs, :]
        scale = lax.rsqrt(jnp.mean(kh * kh, axis=0, keepdims=True) + EPS)
        kt_ref[0, hs, :] = (kh * scale * gk_ref[...]).astype(BF16)


def _kvq(h3, g_kv, g_q, w_kv, w_q, g_k):
    b, s, d = h3.shape
    rows = pl.BlockSpec((1, ROW_TILE, d), lambda bi, i: (bi, i, 0))
    return pl.pallas_call(
        _kvq_kernel,
        out_shape=(jax.ShapeDtypeStruct((b, s, d), BF16),
                   jax.ShapeDtypeStruct((b, d, s), BF16),
                   jax.ShapeDtypeStruct((b, s, d), BF16)),
        grid=(b, s // ROW_TILE),
        in_specs=[rows, _resident((1, d)), _resident((1, d)), _resident((d, 2 * d)),
                  _resident((d, d)), _resident((SB_HEAD_DIM, 1))],
        out_specs=(rows, pl.BlockSpec((1, d, ROW_TILE), lambda bi, i: (bi, 0, i)), rows),
        compiler_params=_params("parallel", "parallel"),
        name="kvq",
    )(h3, g_kv, g_q, w_kv, w_q, g_k)


def _attn_kernel(q_ref, kt_ref, v_ref, gq_ref, suffix_ref, o_ref, acc_ref, later_ref):
    R, KT = ATT_ROWS, ATT_KEYS
    width = q_ref.shape[2]
    heads = width // SB_HEAD_DIM
    q_tiles = q_ref.shape[1] // R
    step = pl.program_id(2)
    lane = lax.broadcasted_iota(jnp.int32, (R, width), 1)
    own = [jnp.logical_and(lane >= hd * SB_HEAD_DIM, lane < (hd + 1) * SB_HEAD_DIM) for hd in range(heads)]
    sign_bit = jnp.uint32(0x80000000)
    row = lax.broadcasted_iota(jnp.int32, (R, R), 0)
    col = lax.broadcasted_iota(jnp.int32, (R, R), 1)
    causal = col < row

    def masked_queries(sub):
        q = q_ref[0, sub * R:(sub + 1) * R, :].astype(F32)
        sq = q * q
        inv = jnp.zeros_like(q)
        for hd in range(heads):
            ms = jnp.sum(jnp.where(own[hd], sq, 0.0), axis=-1, keepdims=True) / SB_HEAD_DIM
            inv = jnp.where(own[hd], lax.rsqrt(ms + EPS), inv)
        qn = q * inv * gq_ref[...] * (LOG2_E / math.sqrt(SB_HEAD_DIM))
        return [jnp.where(own[hd], qn, 0.0).astype(BF16) for hd in range(heads)]

    def head_keys(qm, hd, start, size, diagonal):
        z = _dot(qm, kt_ref[0, :, pl.ds(start, size)])
        if diagonal:
            z = jnp.where(causal, z, MASKED_LOGIT)
        neg_abs = pltpu.bitcast(pltpu.bitcast(z, jnp.uint32) | sign_bit, F32)
        sp = jnp.maximum(z, 0.0) + jnp.log2(1.0 + jnp.exp2(neg_abs))
        cum = _dot(sp.astype(BF16), suffix_ref[0:size, 0:size])
        p = jnp.exp2(z - cum)
        half = (hd * SB_HEAD_DIM) // LANES * LANES
        pv = _dot(p.astype(BF16), v_ref[0, pl.ds(start, size), half:half + LANES])
        return pv, cum[:, 0:1]

    qms, lives = [], []
    for sub in range(q_tiles):
        qi = step * q_tiles + sub
        qm = masked_queries(sub)
        has_before = qi > 0
        before = pl.multiple_of(jnp.maximum(qi * R - KT, 0), R)
        live = None
        for hd in range(heads):
            pv, later = head_keys(qm[hd], hd, pl.multiple_of(qi * R, R), R, True)
            pv_b, tot_b = head_keys(qm[hd], hd, before, KT, False)
            acc_ref[sub, hd] = pv + pv_b * jnp.where(has_before, jnp.exp2(-later), 0.0)
            later = later + jnp.where(has_before, tot_b, 0.0)
            later_ref[sub, hd] = later
            head_live = jnp.min(later)
            live = head_live if live is None else jnp.minimum(live, head_live)
        qms.append(qm)
        lives.append(live)

    def add_keys(sub, start, size):
        live = None
        for hd in range(heads):
            pv, tot = head_keys(qms[sub][hd], hd, start, size, False)
            later = later_ref[sub, hd]
            acc_ref[sub, hd] += pv * jnp.exp2(-later)
            later = later + tot
            later_ref[sub, hd] = later
            head_live = jnp.min(later)
            live = head_live if live is None else jnp.minimum(live, head_live)
        return live

    for sub in range(q_tiles):
        end0 = jnp.maximum((step * q_tiles + sub) * R - KT, 0)

        def cond(carry):
            end, live = carry
            return jnp.logical_and(end >= KT, live < ATT_DEAD_LOG2DECAY)

        def body(carry, sub=sub):
            end, _ = carry
            start = pl.multiple_of(end - KT, KT)
            return start, add_keys(sub, start, KT)

        lax.while_loop(cond, body, (end0, lives[sub]))

    per_half = LANES // SB_HEAD_DIM
    half_lane = lax.broadcasted_iota(jnp.int32, (R, LANES), 1)
    for sub in range(q_tiles):
        for hf in range(width // LANES):
            out = acc_ref[sub, hf * per_half]
            for j in range(1, per_half):
                out = jnp.where(half_lane >= j * SB_HEAD_DIM, acc_ref[sub, hf * per_half + j], out)
            o_ref[0, sub * R:(sub + 1) * R, hf * LANES:(hf + 1) * LANES] = out.astype(BF16)


def _attention(q, k_t, v, g_q):
    b, s, d = q.shape
    assert ATT_KEYS == ATT_ROWS, "the key walk left of the diagonal assumes chunk-aligned query tiles"
    rows = ATT_Q_TILES * ATT_ROWS
    width = ATT_HEADS * SB_HEAD_DIM
    idx = jnp.arange(ATT_KEYS)
    suffix = (idx[:, None] >= idx[None, :]).astype(BF16)
    return pl.pallas_call(
        _attn_kernel,
        out_shape=jax.ShapeDtypeStruct((b, s, d), BF16),
        grid=(b, d // width, s // rows),
        in_specs=[pl.BlockSpec((1, rows, width), lambda bi, hb, i: (bi, i, hb)),
                  pl.BlockSpec((1, width, s), lambda bi, hb, i: (bi, hb, 0)),
                  pl.BlockSpec((1, s, width), lambda bi, hb, i: (bi, 0, hb)),
                  _resident((1, width)), _resident((ATT_KEYS, ATT_KEYS))],
        out_specs=pl.BlockSpec((1, rows, width), lambda bi, hb, i: (bi, i, hb)),
        scratch_shapes=[pltpu.VMEM((ATT_Q_TILES, ATT_HEADS, ATT_ROWS, LANES), F32),
                        pltpu.VMEM((ATT_Q_TILES, ATT_HEADS, ATT_ROWS, 1), F32)],
        compiler_params=_params("parallel", "parallel", "arbitrary"),
        name="attn",
    )(q, k_t, v, g_q, suffix)


def kernel(x, a_norm_g, a_w_in, a_conv_w, a_conv_b, a_dt_bias, a_a_log, a_d_skip, a_gnorm_g, a_w_out,
           kv_norm_g, w_kv, k_norm_g, b_norm_g, b_w_q, b_q_norm_g, b_w_o, mlp_norm_g, w_up, w_down):
    b, s, d = x.shape
    n_a, n_b = a_w_in.shape[0], b_w_q.shape[0]
    n_heads = a_dt_bias.shape[1]

    def lane_pad(v):
        return jnp.pad(v, (0, LANES - v.shape[0]))[None, :]

    h = x.reshape(b * s, d)
    for l in range(n_a):
        w_all = jnp.pad(a_w_in[l], ((0, 0), (0, LANES - n_heads))).astype(BF16)
        z, act, dt_raw = _inproj(h, a_norm_g[l][None, :], w_all, a_conv_w[l], a_conv_b[l][None, :], s)
        h = _ssd(h, z, act, dt_raw, lane_pad(a_dt_bias[l]), lane_pad(a_a_log[l]),
                 jnp.repeat(a_d_skip[l], SSM_HEAD_DIM)[None, :], a_gnorm_g[l][None, :],
                 a_w_out[l].astype(BF16), b)
        h = _mlp(h, mlp_norm_g[l][None, :], w_up[l].astype(BF16), w_down[l].astype(BF16))
    q = k_t = v = None
    for j in range(n_b):
        l = n_a + j
        if j == 0:
            q, k_t, v = _kvq(h.reshape(b, s, d), kv_norm_g[None, :], b_norm_g[j][None, :],
                             w_kv.astype(BF16), b_w_q[j].astype(BF16), k_norm_g[:, None])
        else:
            raise NotImplementedError("one stick-breaking layer per shared K/V projection call")
        attn = _attention(q, k_t, v, jnp.tile(b_q_norm_g[j], ATT_HEADS)[None, :])
        h = _mlp(h, mlp_norm_g[l][None, :], w_up[l].astype(BF16), w_down[l].astype(BF16),
                 attn=attn.reshape(b * s, d), w_o=b_w_o[j].astype(BF16))
    return h.reshape(b, s, d)
```

```python
import functools
import math

import jax
import jax.numpy as jnp
from jax import lax
from jax.experimental import pallas as pl
from jax.experimental.pallas import tpu as pltpu

F32 = jnp.float32
BF16 = jnp.bfloat16
EPS = 1e-5
LOG2_E = 1.4426950408889634

SSM_HEAD_DIM = 64
SSM_GROUPS = 8
D_STATE = 128
D_CONV = 4
SB_HEAD_DIM = 64

LANES = 128
SUBLANES = 8
VMEM_LIMIT_BYTES = 56 * 1024 * 1024

ROW_TILE = 512
INPROJ_ROWS = 256
SSD_CHUNK = 256
FF_TILE = 1024
ATT_ROWS = 256
ATT_KEYS = 256
ATT_HEADS = 4
ATT_Q_TILES = 4
ATT_DEAD_LOG2DECAY = 152.0
MASKED_LOGIT = -1e30
SOFTPLUS_LINEAR = 126.0


def _dot(a, b):
    return jnp.dot(a, b, preferred_element_type=F32)


def _dot_nt(a, b):
    return lax.dot_general(a, b, (((1,), (1,)), ((), ())), preferred_element_type=F32)


def _softplus(x):
    return jnp.maximum(x, 0.0) + jnp.log1p(jnp.exp(-jnp.abs(x)))


def _silu(x):
    return x * jax.nn.sigmoid(x)


def _rms_scale(x):
    return lax.rsqrt(jnp.mean(x * x, axis=-1, keepdims=True) + EPS)


def _resident(shape):
    zeros = (0,) * len(shape)
    return pl.BlockSpec(shape, lambda *_: zeros, pipeline_mode=pl.Buffered(1))


def _params(*semantics):
    return pltpu.CompilerParams(dimension_semantics=semantics, vmem_limit_bytes=VMEM_LIMIT_BYTES)


def _inproj_kernel(x_ref, g_ref, w_ref, convw_ref, convb_ref, z_ref, act_ref, dt_ref, edge,
                   *, steps_per_seq, col_tile):
    tm = x_ref.shape[0]
    d_inner = z_ref.shape[1]
    conv_dim = act_ref.shape[1]
    tail = D_CONV - 1

    @pl.when(pl.program_id(0) % steps_per_seq == 0)
    def _():
        edge[0:SUBLANES, :] = jnp.zeros((SUBLANES, conv_dim), F32)

    x = x_ref[...]
    u = (x * _rms_scale(x) * g_ref[...]).astype(BF16)
    dt_ref[...] = _dot(u, w_ref[:, d_inner + conv_dim:d_inner + conv_dim + LANES])
    n_conv = conv_dim // col_tile
    z_tile = d_inner // n_conv
    for c in range(n_conv):
        zs = slice(c * z_tile, (c + 1) * z_tile)
        z_ref[:, zs] = _dot(u, w_ref[:, zs]).astype(BF16)
        sl = slice(c * col_tile, (c + 1) * col_tile)
        cur = _dot(u, w_ref[:, d_inner + c * col_tile:d_inner + (c + 1) * col_tile])
        w = [convw_ref[k:k + 1, sl] for k in range(D_CONV)]
        prev = pltpu.roll(cur, 1, axis=0)
        acc = pltpu.roll(w[0] * prev + w[1] * cur, 2, axis=0) + (w[2] * prev + w[3] * cur) + convb_ref[:, sl]
        edge[SUBLANES:2 * SUBLANES, sl] = cur[0:SUBLANES]
        head = convb_ref[:, sl] + w[tail] * cur[0:SUBLANES]
        for k in range(tail):
            r0 = SUBLANES - tail + k
            head = head + w[k] * edge[r0:r0 + SUBLANES, sl]
        edge[0:SUBLANES, sl] = cur[tm - SUBLANES:tm]
        act_ref[:, sl] = _silu(jnp.concatenate([head, acc[SUBLANES:tm]], axis=0)).astype(BF16)


def _inproj(h, g, w_all, conv_w, conv_b, seq_len):
    t, d = h.shape
    conv_dim = conv_w.shape[1]
    d_inner = w_all.shape[1] - conv_dim - LANES
    rows = lambda width: pl.BlockSpec((INPROJ_ROWS, width), lambda i: (i, 0))
    return pl.pallas_call(
        functools.partial(_inproj_kernel, steps_per_seq=seq_len // INPROJ_ROWS, col_tile=512),
        out_shape=(jax.ShapeDtypeStruct((t, d_inner), BF16),
                   jax.ShapeDtypeStruct((t, conv_dim), BF16),
                   jax.ShapeDtypeStruct((t, LANES), F32)),
        grid=(t // INPROJ_ROWS,),
        in_specs=[rows(d), _resident((1, d)), _resident(w_all.shape),
                  _resident((D_CONV, conv_dim)), _resident((1, conv_dim))],
        out_specs=(rows(d_inner), rows(conv_dim), rows(LANES)),
        scratch_shapes=[pltpu.VMEM((2 * SUBLANES, conv_dim), F32)],
        compiler_params=_params("arbitrary"),
        name="in_proj",
    )(h, g, w_all, conv_w, conv_b)


def _ssd_kernel(h_ref, z_ref, x_ref, bc_ref, dt_ref, dtb_ref, alog_ref, dskip_ref, gn_ref, expand_ref,
                wout_ref, out_ref, state, ybuf, ecum_x, wgt_x):
    L = SSD_CHUNK
    H = L // 2
    d_inner = x_ref.shape[1]
    n_heads = d_inner // SSM_HEAD_DIM
    heads_per_group = n_heads // SSM_GROUPS
    gw = heads_per_group * SSM_HEAD_DIM
    n_state = SSM_GROUPS * D_STATE

    @pl.when(pl.program_id(1) == 0)
    def _():
        state[...] = jnp.zeros(state.shape, F32)

    dt = _softplus(dt_ref[...] + dtb_ref[...])
    a = dt * (-LOG2_E * jnp.exp(alog_ref[...]))
    row = lax.broadcasted_iota(jnp.int32, (L, L), 0)
    col = lax.broadcasted_iota(jnp.int32, (L, L), 1)
    tri = jnp.where(col <= row, 1.0, 0.0).astype(BF16)
    a_hi = a.astype(BF16)
    r1 = a - a_hi.astype(F32)
    a_mid = r1.astype(BF16)
    a_lo = (r1 - a_mid.astype(F32)).astype(BF16)
    parts = _dot(tri, jnp.concatenate([a_hi, a_mid, a_lo], axis=1))
    cum = parts[:, 0:LANES] + parts[:, LANES:2 * LANES] + parts[:, 2 * LANES:3 * LANES]
    key_t = (cum - jnp.log2(dt)).T
    ecum_x[...] = _dot(jnp.exp2(cum).astype(BF16), expand_ref[...])
    wgt_x[...] = _dot((dt * jnp.exp2(cum[L - 1:L, :] - cum)).astype(BF16), expand_ref[...])

    hrow = lax.broadcasted_iota(jnp.int32, (H, H), 0)
    hcol = lax.broadcasted_iota(jnp.int32, (H, H), 1)
    lower = hcol <= hrow
    glane = lax.broadcasted_iota(jnp.int32, (1, gw), 1)
    head_lanes = [jnp.where(jnp.logical_and(glane >= j * SSM_HEAD_DIM, glane < (j + 1) * SSM_HEAD_DIM),
                            1.0, 0.0).astype(BF16) for j in range(heads_per_group)]

    for g in range(SSM_GROUPS):
        gs = slice(g * gw, (g + 1) * gw)
        xb = x_ref[:, gs]
        x_g = xb.astype(F32)
        b_g = bc_ref[:, g * D_STATE:(g + 1) * D_STATE]
        c_g = bc_ref[:, n_state + g * D_STATE:n_state + (g + 1) * D_STATE]
        cb = _dot_nt(c_g, b_g)
        st = state[g]
        y_top = jnp.zeros((H, gw), F32)
        y_bot = jnp.zeros((H, gw), F32)
        for j in range(heads_per_group):
            hd = g * heads_per_group + j
            xm = xb * head_lanes[j]
            q_i = cum[:, hd:hd + 1]
            k_j = key_t[hd:hd + 1, :]
            d_tl = jnp.exp2(jnp.where(lower, q_i[0:H] - k_j[:, 0:H], -jnp.inf)) * cb[0:H, 0:H]
            d_bl = jnp.exp2(q_i[H:L] - k_j[:, 0:H]) * cb[H:L, 0:H]
            d_br = jnp.exp2(jnp.where(lower, q_i[H:L] - k_j[:, H:L], -jnp.inf)) * cb[H:L, H:L]
            y_top = y_top + _dot(d_tl.astype(BF16), xm[0:H])
            y_bot = y_bot + _dot(jnp.concatenate([d_bl, d_br], axis=1).astype(BF16), xm)
        inter = _dot(c_g, st.astype(BF16)) * ecum_x[:, gs]
        y_g = jnp.concatenate([y_top, y_bot], axis=0) + inter + dskip_ref[:, gs] * x_g
        xw = (x_g * wgt_x[:, gs]).astype(BF16)
        state[g] = st * ecum_x[L - 1:L, gs] + _dot(b_g.astype(F32).T.astype(BF16), xw)
        yg = y_g * _silu(z_ref[:, gs].astype(F32))
        ybuf[:, gs] = (yg * _rms_scale(yg) * gn_ref[:, gs]).astype(BF16)

    out_ref[...] = h_ref[...] + _dot(ybuf[...], wout_ref[...])


def _ssd(h, z, act, dt_raw, dt_bias, a_log, d_skip, gnorm_g, w_out, batch):
    t, d = h.shape
    d_inner = w_out.shape[0]
    assert act.shape[1] == 2 * d_inner, "x and B|C column blocks are addressed with one block width"
    L = SSD_CHUNK
    chunks = t // batch // L
    gw = d_inner // SSM_GROUPS
    expand = (jnp.arange(LANES)[:, None] == jnp.arange(d_inner)[None, :] // SSM_HEAD_DIM).astype(BF16)
    rows = lambda b, c: (b * chunks + c, 0)
    return pl.pallas_call(
        _ssd_kernel,
        out_shape=jax.ShapeDtypeStruct((t, d), F32),
        grid=(batch, chunks),
        in_specs=[pl.BlockSpec((L, d), rows),
                  pl.BlockSpec((L, d_inner), rows),
                  pl.BlockSpec((L, d_inner), rows),
                  pl.BlockSpec((L, d_inner), lambda b, c: (b * chunks + c, 1)),
                  pl.BlockSpec((L, LANES), rows),
                  _resident((1, LANES)), _resident((1, LANES)),
                  _resident((1, d_inner)), _resident((1, d_inner)),
                  _resident((LANES, d_inner)), _resident((d_inner, d))],
        out_specs=pl.BlockSpec((L, d), rows),
        scratch_shapes=[pltpu.VMEM((SSM_GROUPS, D_STATE, gw), F32),
                        pltpu.VMEM((L, d_inner), BF16),
                        pltpu.VMEM((L, d_inner), F32),
                        pltpu.VMEM((L, d_inner), F32)],
        compiler_params=_params("arbitrary", "arbitrary"),
        name="ssd",
    )(h, z, act, act, dt_raw, dt_bias, a_log, d_skip, gnorm_g, expand, w_out)


def _mlp_body(h, g_ref, wup_ref, wdown_ref, out_ref):
    u = (h * _rms_scale(h) * g_ref[...]).astype(BF16)
    acc = h
    for c in range(wup_ref.shape[1] // FF_TILE):
        sl = slice(c * FF_TILE, (c + 1) * FF_TILE)
        a = jnp.maximum(_dot(u, wup_ref[:, sl]), 0.0)
        acc = acc + _dot((a * a).astype(BF16), wdown_ref[sl, :])
    out_ref[...] = acc


def _mlp_kernel(h_ref, g_ref, wup_ref, wdown_ref, out_ref):
    _mlp_body(h_ref[...], g_ref, wup_ref, wdown_ref, out_ref)


def _proj_mlp_kernel(h_ref, a_ref, wo_ref, g_ref, wup_ref, wdown_ref, out_ref):
    _mlp_body(h_ref[...] + _dot(a_ref[...], wo_ref[...]), g_ref, wup_ref, wdown_ref, out_ref)


def _mlp(h, g, w_up, w_down, layer, attn=None, w_o=None):
    t, d = h.shape
    ff = w_up.shape[2]
    rows = pl.BlockSpec((ROW_TILE, d), lambda i: (i, 0))
    stacked = lambda shape: pl.BlockSpec((None,) + shape, lambda i: (layer, 0, 0), pipeline_mode=pl.Buffered(1))
    weights = [_resident((1, d)), stacked((d, ff)), stacked((ff, d))]
    if attn is None:
        body, ins, specs = _mlp_kernel, (h, g, w_up, w_down), [rows] + weights
    else:
        body, ins = _proj_mlp_kernel, (h, attn, w_o, g, w_up, w_down)
        specs = [rows, rows, _resident(w_o.shape)] + weights
    return pl.pallas_call(
        body,
        out_shape=jax.ShapeDtypeStruct((t, d), F32),
        grid=(t // ROW_TILE,),
        in_specs=specs,
        out_specs=rows,
        compiler_params=_params("parallel"),
        name="mlp",
    )(*ins)


def _kvq_kernel(h_ref, gkv_ref, gq_ref, wkv_ref, wq_ref, gk_ref, q_ref, kt_ref, v_ref):
    x = h_ref[0]
    d = x.shape[1]
    xn = x * _rms_scale(x)
    ukv = (xn * gkv_ref[...]).astype(BF16)
    uq = (xn * gq_ref[...]).astype(BF16)
    q_ref[0] = _dot(uq, wq_ref[...]).astype(BF16)
    v_ref[0] = _dot(ukv, wkv_ref[:, d:2 * d]).astype(BF16)
    k_t = _dot(ukv, wkv_ref[:, 0:d]).T
    for hd in range(d // SB_HEAD_DIM):
        hs = slice(hd * SB_HEAD_DIM, (hd + 1) * SB_HEAD_DIM)
        kh = k_t[hs, :]
        scale = lax.rsqrt(jnp.mean(kh * kh, axis=0, keepdims=True) + EPS)
        kt_ref[0, hs, :] = (kh * scale * gk_ref[...]).astype(BF16)


def _kvq(h3, g_kv, g_q, w_kv, w_q, g_k):
    b, s, d = h3.shape
    rows = pl.BlockSpec((1, ROW_TILE, d), lambda bi, i: (bi, i, 0))
    return pl.pallas_call(
        _kvq_kernel,
        out_shape=(jax.ShapeDtypeStruct((b, s, d), BF16),
                   jax.ShapeDtypeStruct((b, d, s), BF16),
                   jax.ShapeDtypeStruct((b, s, d), BF16)),
        grid=(b, s // ROW_TILE),
        in_specs=[rows, _resident((1, d)), _resident((1, d)), _resident((d, 2 * d)),
                  _resident((d, d)), _resident((SB_HEAD_DIM, 1))],
        out_specs=(rows, pl.BlockSpec((1, d, ROW_TILE), lambda bi, i: (bi, 0, i)), rows),
        compiler_params=_params("parallel", "parallel"),
        name="kvq",
    )(h3, g_kv, g_q, w_kv, w_q, g_k)


def _attn_kernel(q_ref, kt_ref, v_ref, gq_ref, suffix_ref, o_ref, acc_ref, later_ref):
    R, KT = ATT_ROWS, ATT_KEYS
    width = q_ref.shape[2]
    heads = width // SB_HEAD_DIM
    q_tiles = q_ref.shape[1] // R
    step = pl.program_id(2)
    lane = lax.broadcasted_iota(jnp.int32, (R, width), 1)
    own = [jnp.logical_and(lane >= hd * SB_HEAD_DIM, lane < (hd + 1) * SB_HEAD_DIM) for hd in range(heads)]
    row = lax.broadcasted_iota(jnp.int32, (R, R), 0)
    col = lax.broadcasted_iota(jnp.int32, (R, R), 1)
    causal = col < row

    def masked_queries(sub):
        q = q_ref[0, sub * R:(sub + 1) * R, :].astype(F32)
        sq = q * q
        inv = jnp.zeros_like(q)
        for hd in range(heads):
            ms = jnp.sum(jnp.where(own[hd], sq, 0.0), axis=-1, keepdims=True) / SB_HEAD_DIM
            inv = jnp.where(own[hd], lax.rsqrt(ms + EPS), inv)
        qn = q * inv * gq_ref[...] * (LOG2_E / math.sqrt(SB_HEAD_DIM))
        return [jnp.where(own[hd], qn, 0.0).astype(BF16) for hd in range(heads)]

    def head_keys(qm, hd, start, size, diagonal):
        z = _dot(qm, kt_ref[0, :, pl.ds(start, size)])
        if diagonal:
            z = jnp.where(causal, z, MASKED_LOGIT)
        sp = jnp.where(z > SOFTPLUS_LINEAR, z, jnp.log2(1.0 + jnp.exp2(z)))
        cum = _dot(sp.astype(BF16), suffix_ref[0:size, 0:size])
        p = jnp.exp2(z - cum)
        half = (hd * SB_HEAD_DIM) // LANES * LANES
        pv = _dot(p.astype(BF16), v_ref[0, pl.ds(start, size), half:half + LANES])
        return pv, cum[:, 0:1]

    qms, lives = [], []
    for sub in range(q_tiles):
        qi = step * q_tiles + sub
        qm = masked_queries(sub)
        has_before = qi > 0
        before = pl.multiple_of(jnp.maximum(qi * R - KT, 0), R)
        live = None
        for hd in range(heads):
            pv, later = head_keys(qm[hd], hd, pl.multiple_of(qi * R, R), R, True)
            pv_b, tot_b = head_keys(qm[hd], hd, before, KT, False)
            acc_ref[sub, hd] = pv + pv_b * jnp.where(has_before, jnp.exp2(-later), 0.0)
            later = later + jnp.where(has_before, tot_b, 0.0)
            later_ref[sub, hd] = later
            head_live = jnp.min(later)
            live = head_live if live is None else jnp.minimum(live, head_live)
        qms.append(qm)
        lives.append(live)

    def add_keys(sub, start, size):
        live = None
        for hd in range(heads):
            pv, tot = head_keys(qms[sub][hd], hd, start, size, False)
            later = later_ref[sub, hd]
            acc_ref[sub, hd] += pv * jnp.exp2(-later)
            later = later + tot
            later_ref[sub, hd] = later
            head_live = jnp.min(later)
            live = head_live if live is None else jnp.minimum(live, head_live)
        return live

    for sub in range(q_tiles):
        end0 = jnp.maximum((step * q_tiles + sub) * R - KT, 0)

        def cond(carry):
            end, live = carry
            return jnp.logical_and(end >= KT, live < ATT_DEAD_LOG2DECAY)

        def body(carry, sub=sub):
            end, _ = carry
            start = pl.multiple_of(end - KT, KT)
            return start, add_keys(sub, start, KT)

        lax.while_loop(cond, body, (end0, lives[sub]))

    per_half = LANES // SB_HEAD_DIM
    half_lane = lax.broadcasted_iota(jnp.int32, (R, LANES), 1)
    for sub in range(q_tiles):
        for hf in range(width // LANES):
            out = acc_ref[sub, hf * per_half]
            for j in range(1, per_half):
                out = jnp.where(half_lane >= j * SB_HEAD_DIM, acc_ref[sub, hf * per_half + j], out)
            o_ref[0, sub * R:(sub + 1) * R, hf * LANES:(hf + 1) * LANES] = out.astype(BF16)


def _attention(q, k_t, v, g_q):
    b, s, d = q.shape
    assert ATT_KEYS == ATT_ROWS, "the key walk left of the diagonal assumes chunk-aligned query tiles"
    rows = ATT_Q_TILES * ATT_ROWS
    width = ATT_HEADS * SB_HEAD_DIM
    idx = jnp.arange(ATT_KEYS)
    suffix = (idx[:, None] >= idx[None, :]).astype(BF16)
    return pl.pallas_call(
        _attn_kernel,
        out_shape=jax.ShapeDtypeStruct((b, s, d), BF16),
        grid=(b, d // width, s // rows),
        in_specs=[pl.BlockSpec((1, rows, width), lambda bi, hb, i: (bi, i, hb)),
                  pl.BlockSpec((1, width, s), lambda bi, hb, i: (bi, hb, 0)),
                  pl.BlockSpec((1, s, width), lambda bi, hb, i: (bi, 0, hb)),
                  _resident((1, width)), _resident((ATT_KEYS, ATT_KEYS))],
        out_specs=pl.BlockSpec((1, rows, width), lambda bi, hb, i: (bi, i, hb)),
        scratch_shapes=[pltpu.VMEM((ATT_Q_TILES, ATT_HEADS, ATT_ROWS, LANES), F32),
                        pltpu.VMEM((ATT_Q_TILES, ATT_HEADS, ATT_ROWS, 1), F32)],
        compiler_params=_params("parallel", "parallel", "arbitrary"),
        name="attn",
    )(q, k_t, v, g_q, suffix)


def kernel(x, a_norm_g, a_w_in, a_conv_w, a_conv_b, a_dt_bias, a_a_log, a_d_skip, a_gnorm_g, a_w_out,
           kv_norm_g, w_kv, k_norm_g, b_norm_g, b_w_q, b_q_norm_g, b_w_o, mlp_norm_g, w_up, w_down):
    b, s, d = x.shape
    n_a, n_b = a_w_in.shape[0], b_w_q.shape[0]
    n_heads = a_dt_bias.shape[1]

    def lane_pad(v):
        return jnp.pad(v, (0, LANES - v.shape[0]))[None, :]

    h = x.reshape(b * s, d)
    w_up_b, w_down_b = w_up.astype(BF16), w_down.astype(BF16)
    for l in range(n_a):
        w_all = jnp.pad(a_w_in[l], ((0, 0), (0, LANES - n_heads))).astype(BF16)
        z, act, dt_raw = _inproj(h, a_norm_g[l][None, :], w_all, a_conv_w[l], a_conv_b[l][None, :], s)
        h = _ssd(h, z, act, dt_raw, lane_pad(a_dt_bias[l]), lane_pad(a_a_log[l]),
                 jnp.repeat(a_d_skip[l], SSM_HEAD_DIM)[None, :], a_gnorm_g[l][None, :],
                 a_w_out[l].astype(BF16), b)
        h = _mlp(h, mlp_norm_g[l][None, :], w_up_b, w_down_b, l)
    q = k_t = v = None
    for j in range(n_b):
        l = n_a + j
        if j == 0:
            q, k_t, v = _kvq(h.reshape(b, s, d), kv_norm_g[None, :], b_norm_g[j][None, :],
                             w_kv.astype(BF16), b_w_q[j].astype(BF16), k_norm_g[:, None])
        else:
            raise NotImplementedError("one stick-breaking layer per shared K/V projection call")
        attn = _attention(q, k_t, v, jnp.tile(b_q_norm_g[j], ATT_HEADS)[None, :])
        h = _mlp(h, mlp_norm_g[l][None, :], w_up_b, w_down_b, l,
                 attn=attn.reshape(b * s, d), w_o=b_w_o[j].astype(BF16))
    return h.reshape(b, s, d)
```

```python
import functools
import math

import jax
import jax.numpy as jnp
from jax import lax
from jax.experimental import pallas as pl
from jax.experimental.pallas import tpu as pltpu

F32 = jnp.float32
BF16 = jnp.bfloat16
EPS = 1e-5
LOG2_E = 1.4426950408889634

SSM_HEAD_DIM = 64
SSM_GROUPS = 8
D_STATE = 128
D_CONV = 4
SB_HEAD_DIM = 64

LANES = 128
SUBLANES = 8
VMEM_LIMIT_BYTES = 56 * 1024 * 1024

ROW_TILE = 512
INPROJ_ROWS = 256
SSD_CHUNK = 256
FF_TILE = 1024
ATT_ROWS = 256
ATT_KEYS = 256
ATT_HEADS = 4
ATT_Q_TILES = 4
ATT_DEAD_LOG2DECAY = 152.0
MASKED_LOGIT = -1e30
SOFTPLUS_LINEAR = 126.0


def _dot(a, b):
    return jnp.dot(a, b, preferred_element_type=F32)


def _dot_nt(a, b):
    return lax.dot_general(a, b, (((1,), (1,)), ((), ())), preferred_element_type=F32)


def _softplus(x):
    return jnp.maximum(x, 0.0) + jnp.log1p(jnp.exp(-jnp.abs(x)))


def _silu(x):
    return x * jax.nn.sigmoid(x)


def _rms_scale(x):
    return lax.rsqrt(jnp.mean(x * x, axis=-1, keepdims=True) + EPS)


def _resident(shape):
    zeros = (0,) * len(shape)
    return pl.BlockSpec(shape, lambda *_: zeros, pipeline_mode=pl.Buffered(1))


def _params(*semantics):
    return pltpu.CompilerParams(dimension_semantics=semantics, vmem_limit_bytes=VMEM_LIMIT_BYTES)


def _inproj_kernel(x_ref, g_ref, w_ref, convw_ref, convb_ref, z_ref, act_ref, dt_ref, edge,
                   *, steps_per_seq, col_tile):
    tm = x_ref.shape[0]
    d_inner = z_ref.shape[1]
    conv_dim = act_ref.shape[1]
    tail = D_CONV - 1

    @pl.when(pl.program_id(0) % steps_per_seq == 0)
    def _():
        edge[0:SUBLANES, :] = jnp.zeros((SUBLANES, conv_dim), F32)

    x = x_ref[...]
    u = (x * _rms_scale(x) * g_ref[...]).astype(BF16)
    dt_ref[...] = _dot(u, w_ref[:, d_inner + conv_dim:d_inner + conv_dim + LANES])
    n_conv = conv_dim // col_tile
    z_tile = d_inner // n_conv
    for c in range(n_conv):
        zs = slice(c * z_tile, (c + 1) * z_tile)
        z_ref[:, zs] = _dot(u, w_ref[:, zs]).astype(BF16)
        sl = slice(c * col_tile, (c + 1) * col_tile)
        cur = _dot(u, w_ref[:, d_inner + c * col_tile:d_inner + (c + 1) * col_tile])
        w = [convw_ref[k:k + 1, sl] for k in range(D_CONV)]
        prev = pltpu.roll(cur, 1, axis=0)
        acc = pltpu.roll(w[0] * prev + w[1] * cur, 2, axis=0) + (w[2] * prev + w[3] * cur) + convb_ref[:, sl]
        edge[SUBLANES:2 * SUBLANES, sl] = cur[0:SUBLANES]
        head = convb_ref[:, sl] + w[tail] * cur[0:SUBLANES]
        for k in range(tail):
            r0 = SUBLANES - tail + k
            head = head + w[k] * edge[r0:r0 + SUBLANES, sl]
        edge[0:SUBLANES, sl] = cur[tm - SUBLANES:tm]
        act_ref[:, sl] = _silu(jnp.concatenate([head, acc[SUBLANES:tm]], axis=0)).astype(BF16)


def _inproj(h, g, w_all, conv_w, conv_b, seq_len):
    t, d = h.shape
    conv_dim = conv_w.shape[1]
    d_inner = w_all.shape[1] - conv_dim - LANES
    rows = lambda width: pl.BlockSpec((INPROJ_ROWS, width), lambda i: (i, 0))
    return pl.pallas_call(
        functools.partial(_inproj_kernel, steps_per_seq=seq_len // INPROJ_ROWS, col_tile=512),
        out_shape=(jax.ShapeDtypeStruct((t, d_inner), BF16),
                   jax.ShapeDtypeStruct((t, conv_dim), BF16),
                   jax.ShapeDtypeStruct((t, LANES), F32)),
        grid=(t // INPROJ_ROWS,),
        in_specs=[rows(d), _resident((1, d)), _resident(w_all.shape),
                  _resident((D_CONV, conv_dim)), _resident((1, conv_dim))],
        out_specs=(rows(d_inner), rows(conv_dim), rows(LANES)),
        scratch_shapes=[pltpu.VMEM((2 * SUBLANES, conv_dim), F32)],
        compiler_params=_params("arbitrary"),
        name="in_proj",
    )(h, g, w_all, conv_w, conv_b)


def _ssd_kernel(h_ref, z_ref, x_ref, bc_ref, dt_ref, dtb_ref, alog_ref, dskip_ref, gn_ref, expand_ref,
                wout_ref, out_ref, state, ybuf, ecum_x, wgt_x):
    L = SSD_CHUNK
    H = L // 2
    d_inner = x_ref.shape[1]
    n_heads = d_inner // SSM_HEAD_DIM
    heads_per_group = n_heads // SSM_GROUPS
    gw = heads_per_group * SSM_HEAD_DIM
    n_state = SSM_GROUPS * D_STATE

    @pl.when(pl.program_id(1) == 0)
    def _():
        state[...] = jnp.zeros(state.shape, F32)

    dt = _softplus(dt_ref[...] + dtb_ref[...])
    a = dt * (-LOG2_E * jnp.exp(alog_ref[...]))
    row = lax.broadcasted_iota(jnp.int32, (L, L), 0)
    col = lax.broadcasted_iota(jnp.int32, (L, L), 1)
    tri = jnp.where(col <= row, 1.0, 0.0).astype(BF16)
    a_hi = a.astype(BF16)
    r1 = a - a_hi.astype(F32)
    a_mid = r1.astype(BF16)
    a_lo = (r1 - a_mid.astype(F32)).astype(BF16)
    parts = _dot(tri, jnp.concatenate([a_hi, a_mid, a_lo], axis=1))
    cum = parts[:, 0:LANES] + parts[:, LANES:2 * LANES] + parts[:, 2 * LANES:3 * LANES]
    key_t = (cum - jnp.log2(dt)).T
    ecum_x[...] = _dot(jnp.exp2(cum).astype(BF16), expand_ref[...])
    wgt_x[...] = _dot((dt * jnp.exp2(cum[L - 1:L, :] - cum)).astype(BF16), expand_ref[...])

    hrow = lax.broadcasted_iota(jnp.int32, (H, H), 0)
    hcol = lax.broadcasted_iota(jnp.int32, (H, H), 1)
    lower = hcol <= hrow
    glane = lax.broadcasted_iota(jnp.int32, (1, gw), 1)
    head_lanes = [jnp.where(jnp.logical_and(glane >= j * SSM_HEAD_DIM, glane < (j + 1) * SSM_HEAD_DIM),
                            1.0, 0.0).astype(BF16) for j in range(heads_per_group)]

    for g in range(SSM_GROUPS):
        gs = slice(g * gw, (g + 1) * gw)
        xb = x_ref[:, gs]
        x_g = xb.astype(F32)
        b_g = bc_ref[:, g * D_STATE:(g + 1) * D_STATE]
        c_g = bc_ref[:, n_state + g * D_STATE:n_state + (g + 1) * D_STATE]
        cb = _dot_nt(c_g, b_g)
        st = state[g]
        y_top = jnp.zeros((H, gw), F32)
        y_bot = jnp.zeros((H, gw), F32)
        for j in range(heads_per_group):
            hd = g * heads_per_group + j
            xm = xb * head_lanes[j]
            q_i = cum[:, hd:hd + 1]
            k_j = key_t[hd:hd + 1, :]
            d_tl = jnp.exp2(jnp.where(lower, q_i[0:H] - k_j[:, 0:H], -jnp.inf)) * cb[0:H, 0:H]
            d_bl = jnp.exp2(q_i[H:L] - k_j[:, 0:H]) * cb[H:L, 0:H]
            d_br = jnp.exp2(jnp.where(lower, q_i[H:L] - k_j[:, H:L], -jnp.inf)) * cb[H:L, H:L]
            y_top = y_top + _dot(d_tl.astype(BF16), xm[0:H])
            y_bot = y_bot + _dot(jnp.concatenate([d_bl, d_br], axis=1).astype(BF16), xm)
        inter = _dot(c_g, st.astype(BF16)) * ecum_x[:, gs]
        y_g = jnp.concatenate([y_top, y_bot], axis=0) + inter + dskip_ref[:, gs] * x_g
        xw = (x_g * wgt_x[:, gs]).astype(BF16)
        state[g] = st * ecum_x[L - 1:L, gs] + _dot(b_g.astype(F32).T.astype(BF16), xw)
        yg = y_g * _silu(z_ref[:, gs].astype(F32))
        ybuf[:, gs] = (yg * _rms_scale(yg) * gn_ref[:, gs]).astype(BF16)

    out_ref[...] = h_ref[...] + _dot(ybuf[...], wout_ref[...])


def _ssd(h, z, act, dt_raw, dt_bias, a_log, d_skip, gnorm_g, w_out, batch):
    t, d = h.shape
    d_inner = w_out.shape[0]
    assert act.shape[1] == 2 * d_inner, "x and B|C column blocks are addressed with one block width"
    L = SSD_CHUNK
    chunks = t // batch // L
    gw = d_inner // SSM_GROUPS
    expand = (jnp.arange(LANES)[:, None] == jnp.arange(d_inner)[None, :] // SSM_HEAD_DIM).astype(BF16)
    rows = lambda b, c: (b * chunks + c, 0)
    return pl.pallas_call(
        _ssd_kernel,
        out_shape=jax.ShapeDtypeStruct((t, d), F32),
        grid=(batch, chunks),
        in_specs=[pl.BlockSpec((L, d), rows),
                  pl.BlockSpec((L, d_inner), rows),
                  pl.BlockSpec((L, d_inner), rows),
                  pl.BlockSpec((L, d_inner), lambda b, c: (b * chunks + c, 1)),
                  pl.BlockSpec((L, LANES), rows),
                  _resident((1, LANES)), _resident((1, LANES)),
                  _resident((1, d_inner)), _resident((1, d_inner)),
                  _resident((LANES, d_inner)), _resident((d_inner, d))],
        out_specs=pl.BlockSpec((L, d), rows),
        scratch_shapes=[pltpu.VMEM((SSM_GROUPS, D_STATE, gw), F32),
                        pltpu.VMEM((L, d_inner), BF16),
                        pltpu.VMEM((L, d_inner), F32),
                        pltpu.VMEM((L, d_inner), F32)],
        compiler_params=_params("arbitrary", "arbitrary"),
        name="ssd",
    )(h, z, act, act, dt_raw, dt_bias, a_log, d_skip, gnorm_g, expand, w_out)


def _mlp_body(h, g_ref, wup_ref, wdown_ref, out_ref):
    u = (h * _rms_scale(h) * g_ref[...]).astype(BF16)
    acc = h
    for c in range(wup_ref.shape[1] // FF_TILE):
        sl = slice(c * FF_TILE, (c + 1) * FF_TILE)
        a = jnp.maximum(_dot(u, wup_ref[:, sl]), 0.0)
        acc = acc + _dot((a * a).astype(BF16), wdown_ref[sl, :])
    out_ref[...] = acc


def _mlp_kernel(h_ref, g_ref, wup_ref, wdown_ref, out_ref):
    _mlp_body(h_ref[...], g_ref, wup_ref, wdown_ref, out_ref)


def _proj_mlp_kernel(h_ref, a_ref, wo_ref, g_ref, wup_ref, wdown_ref, out_ref):
    _mlp_body(h_ref[...] + _dot(a_ref[...], wo_ref[...]), g_ref, wup_ref, wdown_ref, out_ref)


def _mlp(h, g, w_up, w_down, layer, attn=None, w_o=None):
    t, d = h.shape
    ff = w_up.shape[2]
    rows = pl.BlockSpec((ROW_TILE, d), lambda i: (i, 0))
    stacked = lambda shape: pl.BlockSpec((None,) + shape, lambda i: (layer, 0, 0), pipeline_mode=pl.Buffered(1))
    weights = [_resident((1, d)), stacked((d, ff)), stacked((ff, d))]
    if attn is None:
        body, ins, specs = _mlp_kernel, (h, g, w_up, w_down), [rows] + weights
    else:
        body, ins = _proj_mlp_kernel, (h, attn, w_o, g, w_up, w_down)
        specs = [rows, rows, _resident(w_o.shape)] + weights
    return pl.pallas_call(
        body,
        out_shape=jax.ShapeDtypeStruct((t, d), F32),
        grid=(t // ROW_TILE,),
        in_specs=specs,
        out_specs=rows,
        compiler_params=_params("parallel"),
        name="mlp",
    )(*ins)


def _kvq_kernel(h_ref, gkv_ref, gq_ref, wkv_ref, wq_ref, gk_ref, q_ref, kt_ref, v_ref):
    x = h_ref[0]
    d = x.shape[1]
    xn = x * _rms_scale(x)
    ukv = (xn * gkv_ref[...]).astype(BF16)
    uq = (xn * gq_ref[...]).astype(BF16)
    q_ref[0] = _dot(uq, wq_ref[...]).astype(BF16)
    v_ref[0] = _dot(ukv, wkv_ref[:, d:2 * d]).astype(BF16)
    k_t = _dot(ukv, wkv_ref[:, 0:d]).T
    for hd in range(d // SB_HEAD_DIM):
        hs = slice(hd * SB_HEAD_DIM, (hd + 1) * SB_HEAD_DIM)
        kh = k_t[hs, :]
        scale = lax.rsqrt(jnp.mean(kh * kh, axis=0, keepdims=True) + EPS)
        kt_ref[0, hs, :] = (kh * scale * gk_ref[...]).astype(BF16)


def _kvq(h3, g_kv, g_q, w_kv, w_q, g_k):
    b, s, d = h3.shape
    rows = pl.BlockSpec((1, ROW_TILE, d), lambda bi, i: (bi, i, 0))
    return pl.pallas_call(
        _kvq_kernel,
        out_shape=(jax.ShapeDtypeStruct((b, s, d), BF16),
                   jax.ShapeDtypeStruct((b, d, s), BF16),
                   jax.ShapeDtypeStruct((b, s, d), BF16)),
        grid=(b, s // ROW_TILE),
        in_specs=[rows, _resident((1, d)), _resident((1, d)), _resident((d, 2 * d)),
                  _resident((d, d)), _resident((SB_HEAD_DIM, 1))],
        out_specs=(rows, pl.BlockSpec((1, d, ROW_TILE), lambda bi, i: (bi, 0, i)), rows),
        compiler_params=_params("parallel", "parallel"),
        name="kvq",
    )(h3, g_kv, g_q, w_kv, w_q, g_k)


def _attn_kernel(q_ref, kt_ref, v_ref, gq_ref, suffix_ref, o_ref, acc_ref, later_ref):
    R, KT = ATT_ROWS, ATT_KEYS
    width = q_ref.shape[2]
    heads = width // SB_HEAD_DIM
    q_tiles = q_ref.shape[1] // R
    step = pl.program_id(2)
    lane = lax.broadcasted_iota(jnp.int32, (R, width), 1)
    own = [jnp.logical_and(lane >= hd * SB_HEAD_DIM, lane < (hd + 1) * SB_HEAD_DIM) for hd in range(heads)]
    per_half = LANES // SB_HEAD_DIM
    row = lax.broadcasted_iota(jnp.int32, (heads * R, R), 0)
    col = lax.broadcasted_iota(jnp.int32, (heads * R, R), 1)
    causal = col < (row & (R - 1))

    def stacked_queries(sub):
        q = q_ref[0, sub * R:(sub + 1) * R, :].astype(F32)
        sq = q * q
        inv = jnp.zeros_like(q)
        for hd in range(heads):
            ms = jnp.sum(jnp.where(own[hd], sq, 0.0), axis=-1, keepdims=True) / SB_HEAD_DIM
            inv = jnp.where(own[hd], lax.rsqrt(ms + EPS), inv)
        qn = q * inv * gq_ref[...] * (LOG2_E / math.sqrt(SB_HEAD_DIM))
        return jnp.concatenate([jnp.where(own[hd], qn, 0.0).astype(BF16) for hd in range(heads)], axis=0)

    def chunk(qs, start, size, diagonal):
        z = _dot(qs, kt_ref[0, :, pl.ds(start, size)])
        if diagonal:
            z = jnp.where(causal, z, MASKED_LOGIT)
        sp = jnp.where(z > SOFTPLUS_LINEAR, z, jnp.log2(1.0 + jnp.exp2(z)))
        cum = _dot(sp.astype(BF16), suffix_ref[0:size, 0:size])
        p = jnp.exp2(z - cum).astype(BF16)
        pv = [_dot(p[hf * per_half * R:(hf + 1) * per_half * R],
                   v_ref[0, pl.ds(start, size), hf * LANES:(hf + 1) * LANES]) for hf in range(width // LANES)]
        return jnp.concatenate(pv, axis=0), cum[:, 0:1]

    qss, lives = [], []
    for sub in range(q_tiles):
        qi = step * q_tiles + sub
        qs = stacked_queries(sub)
        has_before = qi > 0
        before = pl.multiple_of(jnp.maximum(qi * R - KT, 0), R)
        pv, later = chunk(qs, pl.multiple_of(qi * R, R), R, True)
        pv_b, tot_b = chunk(qs, before, KT, False)
        acc_ref[sub] = pv + pv_b * jnp.where(has_before, jnp.exp2(-later), 0.0)
        later = later + jnp.where(has_before, tot_b, 0.0)
        later_ref[sub] = later
        qss.append(qs)
        lives.append(jnp.min(later))

    for sub in range(q_tiles):
        end0 = jnp.maximum((step * q_tiles + sub) * R - KT, 0)

        def cond(carry):
            end, live = carry
            return jnp.logical_and(end >= KT, live < ATT_DEAD_LOG2DECAY)

        def body(carry, sub=sub):
            end, _ = carry
            start = pl.multiple_of(end - KT, KT)
            pv, tot = chunk(qss[sub], start, KT, False)
            later = later_ref[sub]
            acc_ref[sub] += pv * jnp.exp2(-later)
            later = later + tot
            later_ref[sub] = later
            return start, jnp.min(later)

        lax.while_loop(cond, body, (end0, lives[sub]))

    half_lane = lax.broadcasted_iota(jnp.int32, (R, LANES), 1)
    for sub in range(q_tiles):
        for hf in range(width // LANES):
            out = acc_ref[sub, hf * per_half * R:(hf * per_half + 1) * R]
            for j in range(1, per_half):
                out = jnp.where(half_lane >= j * SB_HEAD_DIM,
                                acc_ref[sub, (hf * per_half + j) * R:(hf * per_half + j + 1) * R], out)
            o_ref[0, sub * R:(sub + 1) * R, hf * LANES:(hf + 1) * LANES] = out.astype(BF16)


def _attention(q, k_t, v, g_q):
    b, s, d = q.shape
    assert ATT_KEYS == ATT_ROWS, "the key walk left of the diagonal assumes chunk-aligned query tiles"
    rows = ATT_Q_TILES * ATT_ROWS
    width = ATT_HEADS * SB_HEAD_DIM
    idx = jnp.arange(ATT_KEYS)
    suffix = (idx[:, None] >= idx[None, :]).astype(BF16)
    return pl.pallas_call(
        _attn_kernel,
        out_shape=jax.ShapeDtypeStruct((b, s, d), BF16),
        grid=(b, d // width, s // rows),
        in_specs=[pl.BlockSpec((1, rows, width), lambda bi, hb, i: (bi, i, hb)),
                  pl.BlockSpec((1, width, s), lambda bi, hb, i: (bi, hb, 0)),
                  pl.BlockSpec((1, s, width), lambda bi, hb, i: (bi, 0, hb)),
                  _resident((1, width)), _resident((ATT_KEYS, ATT_KEYS))],
        out_specs=pl.BlockSpec((1, rows, width), lambda bi, hb, i: (bi, i, hb)),
        scratch_shapes=[pltpu.VMEM((ATT_Q_TILES, ATT_HEADS * ATT_ROWS, LANES), F32),
                        pltpu.VMEM((ATT_Q_TILES, ATT_HEADS * ATT_ROWS, 1), F32)],
        compiler_params=_params("parallel", "parallel", "arbitrary"),
        name="attn",
    )(q, k_t, v, g_q, suffix)


def kernel(x, a_norm_g, a_w_in, a_conv_w, a_conv_b, a_dt_bias, a_a_log, a_d_skip, a_gnorm_g, a_w_out,
           kv_norm_g, w_kv, k_norm_g, b_norm_g, b_w_q, b_q_norm_g, b_w_o, mlp_norm_g, w_up, w_down):
    b, s, d = x.shape
    n_a, n_b = a_w_in.shape[0], b_w_q.shape[0]
    n_heads = a_dt_bias.shape[1]

    def lane_pad(v):
        return jnp.pad(v, (0, LANES - v.shape[0]))[None, :]

    h = x.reshape(b * s, d)
    w_up_b, w_down_b = w_up.astype(BF16), w_down.astype(BF16)
    for l in range(n_a):
        w_all = jnp.pad(a_w_in[l], ((0, 0), (0, LANES - n_heads))).astype(BF16)
        z, act, dt_raw = _inproj(h, a_norm_g[l][None, :], w_all, a_conv_w[l], a_conv_b[l][None, :], s)
        h = _ssd(h, z, act, dt_raw, lane_pad(a_dt_bias[l]), lane_pad(a_a_log[l]),
                 jnp.repeat(a_d_skip[l], SSM_HEAD_DIM)[None, :], a_gnorm_g[l][None, :],
                 a_w_out[l].astype(BF16), b)
        h = _mlp(h, mlp_norm_g[l][None, :], w_up_b, w_down_b, l)
    q = k_t = v = None
    for j in range(n_b):
        l = n_a + j
        if j == 0:
            q, k_t, v = _kvq(h.reshape(b, s, d), kv_norm_g[None, :], b_norm_g[j][None, :],
                             w_kv.astype(BF16), b_w_q[j].astype(BF16), k_norm_g[:, None])
        else:
            raise NotImplementedError("one stick-breaking layer per shared K/V projection call")
        attn = _attention(q, k_t, v, jnp.tile(b_q_norm_g[j], ATT_HEADS)[None, :])
        h = _mlp(h, mlp_norm_g[l][None, :], w_up_b, w_down_b, l,
                 attn=attn.reshape(b * s, d), w_o=b_w_o[j].astype(BF16))
    return h.reshape(b, s, d)
```

```python
import functools
import math

import jax
import jax.numpy as jnp
from jax import lax
from jax.experimental import pallas as pl
from jax.experimental.pallas import tpu as pltpu

F32 = jnp.float32
BF16 = jnp.bfloat16
EPS = 1e-5
LOG2_E = 1.4426950408889634

SSM_HEAD_DIM = 64
SSM_GROUPS = 8
D_STATE = 128
D_CONV = 4
SB_HEAD_DIM = 64

LANES = 128
SUBLANES = 8
VMEM_LIMIT_BYTES = 56 * 1024 * 1024

ROW_TILE = 512
INPROJ_ROWS = 256
SSD_CHUNK = 256
FF_TILE = 1024
ATT_ROWS = 256
ATT_KEYS = 256
ATT_HEADS = 4
ATT_Q_TILES = 4
ATT_DEAD_LOG2DECAY = 152.0
MASKED_LOGIT = -1e30
SOFTPLUS_LINEAR = 126.0


def _dot(a, b):
    return jnp.dot(a, b, preferred_element_type=F32)


def _dot_nt(a, b):
    return lax.dot_general(a, b, (((1,), (1,)), ((), ())), preferred_element_type=F32)


def _softplus(x):
    return jnp.maximum(x, 0.0) + jnp.log1p(jnp.exp(-jnp.abs(x)))


def _silu(x):
    return x * jax.nn.sigmoid(x)


def _rms_scale(x):
    return lax.rsqrt(jnp.mean(x * x, axis=-1, keepdims=True) + EPS)


def _resident(shape):
    zeros = (0,) * len(shape)
    return pl.BlockSpec(shape, lambda *_: zeros, pipeline_mode=pl.Buffered(1))


def _params(*semantics):
    return pltpu.CompilerParams(dimension_semantics=semantics, vmem_limit_bytes=VMEM_LIMIT_BYTES)


def _inproj_kernel(x_ref, g_ref, w_ref, convw_ref, convb_ref, z_ref, act_ref, dt_ref, edge,
                   *, steps_per_seq, col_tile):
    tm = x_ref.shape[0]
    d_inner = z_ref.shape[1]
    conv_dim = act_ref.shape[1]
    tail = D_CONV - 1

    @pl.when(pl.program_id(0) % steps_per_seq == 0)
    def _():
        edge[0:SUBLANES, :] = jnp.zeros((SUBLANES, conv_dim), F32)

    x = x_ref[...]
    u = (x * _rms_scale(x) * g_ref[...]).astype(BF16)
    dt_ref[...] = _dot(u, w_ref[:, d_inner + conv_dim:d_inner + conv_dim + LANES])
    n_conv = conv_dim // col_tile
    z_tile = d_inner // n_conv
    for c in range(n_conv):
        zs = slice(c * z_tile, (c + 1) * z_tile)
        z_ref[:, zs] = _dot(u, w_ref[:, zs]).astype(BF16)
        sl = slice(c * col_tile, (c + 1) * col_tile)
        cur = _dot(u, w_ref[:, d_inner + c * col_tile:d_inner + (c + 1) * col_tile])
        w = [convw_ref[k:k + 1, sl] for k in range(D_CONV)]
        prev = pltpu.roll(cur, 1, axis=0)
        acc = pltpu.roll(w[0] * prev + w[1] * cur, 2, axis=0) + (w[2] * prev + w[3] * cur) + convb_ref[:, sl]
        edge[SUBLANES:2 * SUBLANES, sl] = cur[0:SUBLANES]
        head = convb_ref[:, sl] + w[tail] * cur[0:SUBLANES]
        for k in range(tail):
            r0 = SUBLANES - tail + k
            head = head + w[k] * edge[r0:r0 + SUBLANES, sl]
        edge[0:SUBLANES, sl] = cur[tm - SUBLANES:tm]
        act_ref[:, sl] = _silu(jnp.concatenate([head, acc[SUBLANES:tm]], axis=0)).astype(BF16)


def _inproj(h, g, w_all, conv_w, conv_b, seq_len):
    t, d = h.shape
    conv_dim = conv_w.shape[1]
    d_inner = w_all.shape[1] - conv_dim - LANES
    rows = lambda width: pl.BlockSpec((INPROJ_ROWS, width), lambda i: (i, 0))
    return pl.pallas_call(
        functools.partial(_inproj_kernel, steps_per_seq=seq_len // INPROJ_ROWS, col_tile=512),
        out_shape=(jax.ShapeDtypeStruct((t, d_inner), BF16),
                   jax.ShapeDtypeStruct((t, conv_dim), BF16),
                   jax.ShapeDtypeStruct((t, LANES), F32)),
        grid=(t // INPROJ_ROWS,),
        in_specs=[rows(d), _resident((1, d)), _resident(w_all.shape),
                  _resident((D_CONV, conv_dim)), _resident((1, conv_dim))],
        out_specs=(rows(d_inner), rows(conv_dim), rows(LANES)),
        scratch_shapes=[pltpu.VMEM((2 * SUBLANES, conv_dim), F32)],
        compiler_params=_params("arbitrary"),
        name="in_proj",
    )(h, g, w_all, conv_w, conv_b)


def _ssd_kernel(h_ref, z_ref, x_ref, bc_ref, dt_ref, dtb_ref, alog_ref, dskip_ref, gn_ref, expand_ref,
                wout_ref, out_ref, state, ybuf, ecum_x, wgt_x):
    L = SSD_CHUNK
    H = L // 2
    d_inner = x_ref.shape[1]
    n_heads = d_inner // SSM_HEAD_DIM
    heads_per_group = n_heads // SSM_GROUPS
    gw = heads_per_group * SSM_HEAD_DIM
    n_state = SSM_GROUPS * D_STATE

    @pl.when(pl.program_id(1) == 0)
    def _():
        state[...] = jnp.zeros(state.shape, F32)

    dt = _softplus(dt_ref[...] + dtb_ref[...])
    a = dt * (-LOG2_E * jnp.exp(alog_ref[...]))
    row = lax.broadcasted_iota(jnp.int32, (L, L), 0)
    col = lax.broadcasted_iota(jnp.int32, (L, L), 1)
    tri = jnp.where(col <= row, 1.0, 0.0).astype(BF16)
    a_hi = a.astype(BF16)
    r1 = a - a_hi.astype(F32)
    a_mid = r1.astype(BF16)
    a_lo = (r1 - a_mid.astype(F32)).astype(BF16)
    parts = _dot(tri, jnp.concatenate([a_hi, a_mid, a_lo], axis=1))
    cum = parts[:, 0:LANES] + parts[:, LANES:2 * LANES] + parts[:, 2 * LANES:3 * LANES]
    key_t = (cum - jnp.log2(dt)).T
    per_head = jnp.concatenate([jnp.exp2(cum), dt * jnp.exp2(cum[L - 1:L, :] - cum)], axis=0).astype(BF16)
    per_channel = _dot(per_head, expand_ref[...])
    ecum_x[...] = per_channel[0:L]
    wgt_x[...] = per_channel[L:2 * L]

    hrow = lax.broadcasted_iota(jnp.int32, (H, H), 0)
    hcol = lax.broadcasted_iota(jnp.int32, (H, H), 1)
    lower = hcol <= hrow
    glane = lax.broadcasted_iota(jnp.int32, (H, gw), 1)

    def own_channels(stacked):
        out = stacked[0:H]
        for j in range(1, heads_per_group):
            out = jnp.where(glane >= j * SSM_HEAD_DIM, stacked[j * H:(j + 1) * H], out)
        return out

    for g in range(SSM_GROUPS):
        gs = slice(g * gw, (g + 1) * gw)
        xb = x_ref[:, gs]
        x_g = xb.astype(F32)
        b_g = bc_ref[:, g * D_STATE:(g + 1) * D_STATE]
        c_g = bc_ref[:, n_state + g * D_STATE:n_state + (g + 1) * D_STATE]
        st = state[g]
        b_t = b_g.astype(F32).T.astype(BF16)
        both = _dot(c_g, jnp.concatenate([b_t, st.astype(BF16)], axis=1))
        cb = both[:, 0:L]
        tops, bots = [], []
        for j in range(heads_per_group):
            hd = g * heads_per_group + j
            q_i = cum[:, hd:hd + 1]
            k_j = key_t[hd:hd + 1, :]
            d_tl = jnp.exp2(jnp.where(lower, q_i[0:H] - k_j[:, 0:H], -jnp.inf)) * cb[0:H, 0:H]
            d_bl = jnp.exp2(q_i[H:L] - k_j[:, 0:H]) * cb[H:L, 0:H]
            d_br = jnp.exp2(jnp.where(lower, q_i[H:L] - k_j[:, H:L], -jnp.inf)) * cb[H:L, H:L]
            tops.append(d_tl.astype(BF16))
            bots.append(jnp.concatenate([d_bl, d_br], axis=1).astype(BF16))
        y_top = own_channels(_dot(jnp.concatenate(tops, axis=0), xb[0:H]))
        y_bot = own_channels(_dot(jnp.concatenate(bots, axis=0), xb))
        inter = both[:, L:L + gw] * ecum_x[:, gs]
        y_g = jnp.concatenate([y_top, y_bot], axis=0) + inter + dskip_ref[:, gs] * x_g
        xw = (x_g * wgt_x[:, gs]).astype(BF16)
        state[g] = st * ecum_x[L - 1:L, gs] + _dot(b_t, xw)
        yg = y_g * _silu(z_ref[:, gs].astype(F32))
        ybuf[:, gs] = (yg * _rms_scale(yg) * gn_ref[:, gs]).astype(BF16)

    out_ref[...] = h_ref[...] + _dot(ybuf[...], wout_ref[...])


def _ssd(h, z, act, dt_raw, dt_bias, a_log, d_skip, gnorm_g, w_out, batch):
    t, d = h.shape
    d_inner = w_out.shape[0]
    assert act.shape[1] == 2 * d_inner, "x and B|C column blocks are addressed with one block width"
    L = SSD_CHUNK
    chunks = t // batch // L
    gw = d_inner // SSM_GROUPS
    expand = (jnp.arange(LANES)[:, None] == jnp.arange(d_inner)[None, :] // SSM_HEAD_DIM).astype(BF16)
    rows = lambda b, c: (b * chunks + c, 0)
    return pl.pallas_call(
        _ssd_kernel,
        out_shape=jax.ShapeDtypeStruct((t, d), F32),
        grid=(batch, chunks),
        in_specs=[pl.BlockSpec((L, d), rows),
                  pl.BlockSpec((L, d_inner), rows),
                  pl.BlockSpec((L, d_inner), rows),
                  pl.BlockSpec((L, d_inner), lambda b, c: (b * chunks + c, 1)),
                  pl.BlockSpec((L, LANES), rows),
                  _resident((1, LANES)), _resident((1, LANES)),
                  _resident((1, d_inner)), _resident((1, d_inner)),
                  _resident((LANES, d_inner)), _resident((d_inner, d))],
        out_specs=pl.BlockSpec((L, d), rows),
        scratch_shapes=[pltpu.VMEM((SSM_GROUPS, D_STATE, gw), F32),
                        pltpu.VMEM((L, d_inner), BF16),
                        pltpu.VMEM((L, d_inner), F32),
                        pltpu.VMEM((L, d_inner), F32)],
        compiler_params=_params("arbitrary", "arbitrary"),
        name="ssd",
    )(h, z, act, act, dt_raw, dt_bias, a_log, d_skip, gnorm_g, expand, w_out)


def _mlp_body(h, g_ref, wup_ref, wdown_ref, out_ref):
    u = (h * _rms_scale(h) * g_ref[...]).astype(BF16)
    acc = h
    for c in range(wup_ref.shape[1] // FF_TILE):
        sl = slice(c * FF_TILE, (c + 1) * FF_TILE)
        a = jnp.maximum(_dot(u, wup_ref[:, sl]), 0.0)
        acc = acc + _dot((a * a).astype(BF16), wdown_ref[sl, :])
    out_ref[...] = acc


def _mlp_kernel(h_ref, g_ref, wup_ref, wdown_ref, out_ref):
    _mlp_body(h_ref[...], g_ref, wup_ref, wdown_ref, out_ref)


def _proj_mlp_kernel(h_ref, a_ref, wo_ref, g_ref, wup_ref, wdown_ref, out_ref):
    _mlp_body(h_ref[...] + _dot(a_ref[...], wo_ref[...]), g_ref, wup_ref, wdown_ref, out_ref)


def _mlp(h, g, w_up, w_down, layer, attn=None, w_o=None):
    t, d = h.shape
    ff = w_up.shape[2]
    rows = pl.BlockSpec((ROW_TILE, d), lambda i: (i, 0))
    stacked = lambda shape: pl.BlockSpec((None,) + shape, lambda i: (layer, 0, 0), pipeline_mode=pl.Buffered(1))
    weights = [_resident((1, d)), stacked((d, ff)), stacked((ff, d))]
    if attn is None:
        body, ins, specs = _mlp_kernel, (h, g, w_up, w_down), [rows] + weights
    else:
        body, ins = _proj_mlp_kernel, (h, attn, w_o, g, w_up, w_down)
        specs = [rows, rows, _resident(w_o.shape)] + weights
    return pl.pallas_call(
        body,
        out_shape=jax.ShapeDtypeStruct((t, d), F32),
        grid=(t // ROW_TILE,),
        in_specs=specs,
        out_specs=rows,
        compiler_params=_params("parallel"),
        name="mlp",
    )(*ins)


def _kvq_kernel(h_ref, gkv_ref, gq_ref, wkv_ref, wq_ref, gk_ref, q_ref, kt_ref, v_ref):
    x = h_ref[0]
    d = x.shape[1]
    xn = x * _rms_scale(x)
    ukv = (xn * gkv_ref[...]).astype(BF16)
    uq = (xn * gq_ref[...]).astype(BF16)
    k_t = _dot(ukv, wkv_ref[:, 0:d]).T
    q_ref[0] = _dot(uq, wq_ref[...]).astype(BF16)
    v_ref[0] = _dot(ukv, wkv_ref[:, d:2 * d]).astype(BF16)
    for hd in range(d // SB_HEAD_DIM):
        hs = slice(hd * SB_HEAD_DIM, (hd + 1) * SB_HEAD_DIM)
        kh = k_t[hs, :]
        scale = lax.rsqrt(jnp.mean(kh * kh, axis=0, keepdims=True) + EPS)
        kt_ref[0, hs, :] = (kh * scale * gk_ref[...]).astype(BF16)


def _kvq(h3, g_kv, g_q, w_kv, w_q, g_k):
    b, s, d = h3.shape
    rows = pl.BlockSpec((1, ROW_TILE, d), lambda bi, i: (bi, i, 0))
    return pl.pallas_call(
        _kvq_kernel,
        out_shape=(jax.ShapeDtypeStruct((b, s, d), BF16),
                   jax.ShapeDtypeStruct((b, d, s), BF16),
                   jax.ShapeDtypeStruct((b, s, d), BF16)),
        grid=(b, s // ROW_TILE),
        in_specs=[rows, _resident((1, d)), _resident((1, d)), _resident((d, 2 * d)),
                  _resident((d, d)), _resident((SB_HEAD_DIM, 1))],
        out_specs=(rows, pl.BlockSpec((1, d, ROW_TILE), lambda bi, i: (bi, 0, i)), rows),
        compiler_params=_params("parallel", "parallel"),
        name="kvq",
    )(h3, g_kv, g_q, w_kv, w_q, g_k)


def _attn_kernel(q_ref, kt_ref, v_ref, gq_ref, suffix_ref, o_ref, acc_ref, later_ref):
    R, KT = ATT_ROWS, ATT_KEYS
    width = q_ref.shape[2]
    heads = width // SB_HEAD_DIM
    q_tiles = q_ref.shape[1] // R
    step = pl.program_id(2)
    lane = lax.broadcasted_iota(jnp.int32, (R, width), 1)
    own = [jnp.logical_and(lane >= hd * SB_HEAD_DIM, lane < (hd + 1) * SB_HEAD_DIM) for hd in range(heads)]
    per_half = LANES // SB_HEAD_DIM
    row = lax.broadcasted_iota(jnp.int32, (heads * R, R), 0)
    col = lax.broadcasted_iota(jnp.int32, (heads * R, R), 1)
    causal = col < (row & (R - 1))

    def stacked_queries(sub):
        q = q_ref[0, sub * R:(sub + 1) * R, :].astype(F32)
        sq = q * q
        inv = jnp.zeros_like(q)
        for hd in range(heads):
            ms = jnp.sum(jnp.where(own[hd], sq, 0.0), axis=-1, keepdims=True) / SB_HEAD_DIM
            inv = jnp.where(own[hd], lax.rsqrt(ms + EPS), inv)
        qn = q * inv * gq_ref[...] * (LOG2_E / math.sqrt(SB_HEAD_DIM))
        return jnp.concatenate([jnp.where(own[hd], qn, 0.0).astype(BF16) for hd in range(heads)], axis=0)

    def chunk(qs, start, size, diagonal):
        z = _dot(qs, kt_ref[0, :, pl.ds(start, size)])
        if diagonal:
            z = jnp.where(causal, z, MASKED_LOGIT)
        sp = jnp.where(z > SOFTPLUS_LINEAR, z, jnp.log2(1.0 + jnp.exp2(z)))
        cum = _dot(sp.astype(BF16), suffix_ref[0:size, 0:size])
        p = jnp.exp2(z - cum).astype(BF16)
        pv = [_dot(p[hf * per_half * R:(hf + 1) * per_half * R],
                   v_ref[0, pl.ds(start, size), hf * LANES:(hf + 1) * LANES]) for hf in range(width // LANES)]
        return jnp.concatenate(pv, axis=0), cum[:, 0:1]

    qss, lives = [], []
    for sub in range(q_tiles):
        qi = step * q_tiles + sub
        qs = stacked_queries(sub)
        has_before = qi > 0
        before = pl.multiple_of(jnp.maximum(qi * R - KT, 0), R)
        pv, later = chunk(qs, pl.multiple_of(qi * R, R), R, True)
        pv_b, tot_b = chunk(qs, before, KT, False)
        acc_ref[sub] = pv + pv_b * jnp.where(has_before, jnp.exp2(-later), 0.0)
        later = later + jnp.where(has_before, tot_b, 0.0)
        later_ref[sub] = later
        qss.append(qs)
        lives.append(jnp.min(later))

    for sub in range(q_tiles):
        end0 = jnp.maximum((step * q_tiles + sub) * R - KT, 0)

        def cond(carry):
            end, live = carry
            return jnp.logical_and(end >= KT, live < ATT_DEAD_LOG2DECAY)

        def body(carry, sub=sub):
            end, _ = carry
            start = pl.multiple_of(end - KT, KT)
            pv, tot = chunk(qss[sub], start, KT, False)
            later = later_ref[sub]
            acc_ref[sub] += pv * jnp.exp2(-later)
            later = later + tot
            later_ref[sub] = later
            return start, jnp.min(later)

        lax.while_loop(cond, body, (end0, lives[sub]))

    half_lane = lax.broadcasted_iota(jnp.int32, (R, LANES), 1)
    for sub in range(q_tiles):
        for hf in range(width // LANES):
            out = acc_ref[sub, hf * per_half * R:(hf * per_half + 1) * R]
            for j in range(1, per_half):
                out = jnp.where(half_lane >= j * SB_HEAD_DIM,
                                acc_ref[sub, (hf * per_half + j) * R:(hf * per_half + j + 1) * R], out)
            o_ref[0, sub * R:(sub + 1) * R, hf * LANES:(hf + 1) * LANES] = out.astype(BF16)


def _attention(q, k_t, v, g_q):
    b, s, d = q.shape
    assert ATT_KEYS == ATT_ROWS, "the key walk left of the diagonal assumes chunk-aligned query tiles"
    rows = ATT_Q_TILES * ATT_ROWS
    width = ATT_HEADS * SB_HEAD_DIM
    idx = jnp.arange(ATT_KEYS)
    suffix = (idx[:, None] >= idx[None, :]).astype(BF16)
    return pl.pallas_call(
        _attn_kernel,
        out_shape=jax.ShapeDtypeStruct((b, s, d), BF16),
        grid=(b, d // width, s // rows),
        in_specs=[pl.BlockSpec((1, rows, width), lambda bi, hb, i: (bi, i, hb)),
                  pl.BlockSpec((1, width, s), lambda bi, hb, i: (bi, hb, 0)),
                  pl.BlockSpec((1, s, width), lambda bi, hb, i: (bi, 0, hb)),
                  _resident((1, width)), _resident((ATT_KEYS, ATT_KEYS))],
        out_specs=pl.BlockSpec((1, rows, width), lambda bi, hb, i: (bi, i, hb)),
        scratch_shapes=[pltpu.VMEM((ATT_Q_TILES, ATT_HEADS * ATT_ROWS, LANES), F32),
                        pltpu.VMEM((ATT_Q_TILES, ATT_HEADS * ATT_ROWS, 1), F32)],
        compiler_params=_params("parallel", "parallel", "arbitrary"),
        name="attn",
    )(q, k_t, v, g_q, suffix)


def kernel(x, a_norm_g, a_w_in, a_conv_w, a_conv_b, a_dt_bias, a_a_log, a_d_skip, a_gnorm_g, a_w_out,
           kv_norm_g, w_kv, k_norm_g, b_norm_g, b_w_q, b_q_norm_g, b_w_o, mlp_norm_g, w_up, w_down):
    b, s, d = x.shape
    n_a, n_b = a_w_in.shape[0], b_w_q.shape[0]
    n_heads = a_dt_bias.shape[1]

    def lane_pad(v):
        return jnp.pad(v, (0, LANES - v.shape[0]))[None, :]

    h = x.reshape(b * s, d)
    w_up_b, w_down_b = w_up.astype(BF16), w_down.astype(BF16)
    for l in range(n_a):
        w_all = jnp.pad(a_w_in[l], ((0, 0), (0, LANES - n_heads))).astype(BF16)
        z, act, dt_raw = _inproj(h, a_norm_g[l][None, :], w_all, a_conv_w[l], a_conv_b[l][None, :], s)
        h = _ssd(h, z, act, dt_raw, lane_pad(a_dt_bias[l]), lane_pad(a_a_log[l]),
                 jnp.repeat(a_d_skip[l], SSM_HEAD_DIM)[None, :], a_gnorm_g[l][None, :],
                 a_w_out[l].astype(BF16), b)
        h = _mlp(h, mlp_norm_g[l][None, :], w_up_b, w_down_b, l)
    q = k_t = v = None
    for j in range(n_b):
        l = n_a + j
        if j == 0:
            q, k_t, v = _kvq(h.reshape(b, s, d), kv_norm_g[None, :], b_norm_g[j][None, :],
                             w_kv.astype(BF16), b_w_q[j].astype(BF16), k_norm_g[:, None])
        else:
            raise NotImplementedError("one stick-breaking layer per shared K/V projection call")
        attn = _attention(q, k_t, v, jnp.tile(b_q_norm_g[j], ATT_HEADS)[None, :])
        h = _mlp(h, mlp_norm_g[l][None, :], w_up_b, w_down_b, l,
                 attn=attn.reshape(b * s, d), w_o=b_w_o[j].astype(BF16))
    return h.reshape(b, s, d)
```

```python
import functools
import math

import jax
import jax.numpy as jnp
from jax import lax
from jax.experimental import pallas as pl
from jax.experimental.pallas import tpu as pltpu

F32 = jnp.float32
BF16 = jnp.bfloat16
EPS = 1e-5
LOG2_E = 1.4426950408889634

SSM_HEAD_DIM = 64
SSM_GROUPS = 8
D_STATE = 128
D_CONV = 4
SB_HEAD_DIM = 64

LANES = 128
SUBLANES = 8
VMEM_LIMIT_BYTES = 56 * 1024 * 1024

ROW_TILE = 1024
INPROJ_ROWS = 256
SSD_CHUNK = 256
FF_TILE = 1024
ATT_ROWS = 256
ATT_KEYS = 256
ATT_HEADS = 4
ATT_Q_TILES = 4
ATT_DEAD_LOG2DECAY = 152.0
MASKED_LOGIT = -1e30
SOFTPLUS_LINEAR = 126.0


def _dot(a, b):
    return jnp.dot(a, b, preferred_element_type=F32)


def _dot_nt(a, b):
    return lax.dot_general(a, b, (((1,), (1,)), ((), ())), preferred_element_type=F32)


def _softplus(x):
    return jnp.maximum(x, 0.0) + jnp.log1p(jnp.exp(-jnp.abs(x)))


def _silu(x):
    return x * jax.nn.sigmoid(x)


def _rms_scale(x):
    return lax.rsqrt(jnp.mean(x * x, axis=-1, keepdims=True) + EPS)


def _resident(shape):
    zeros = (0,) * len(shape)
    return pl.BlockSpec(shape, lambda *_: zeros, pipeline_mode=pl.Buffered(1))


def _params(*semantics):
    return pltpu.CompilerParams(dimension_semantics=semantics, vmem_limit_bytes=VMEM_LIMIT_BYTES)


def _inproj_kernel(x_ref, g_ref, w_ref, convw_ref, convb_ref, z_ref, act_ref, dt_ref, edge,
                   *, steps_per_seq, col_tile):
    tm = x_ref.shape[0]
    d_inner = z_ref.shape[1]
    conv_dim = act_ref.shape[1]
    tail = D_CONV - 1

    @pl.when(pl.program_id(0) % steps_per_seq == 0)
    def _():
        edge[0:SUBLANES, :] = jnp.zeros((SUBLANES, conv_dim), F32)

    x = x_ref[...]
    u = (x * _rms_scale(x) * g_ref[...]).astype(BF16)
    dt_ref[...] = _dot(u, w_ref[:, d_inner + conv_dim:d_inner + conv_dim + LANES])
    n_conv = conv_dim // col_tile
    z_tile = d_inner // n_conv
    for c in range(n_conv):
        zs = slice(c * z_tile, (c + 1) * z_tile)
        z_ref[:, zs] = _dot(u, w_ref[:, zs]).astype(BF16)
        sl = slice(c * col_tile, (c + 1) * col_tile)
        cur = _dot(u, w_ref[:, d_inner + c * col_tile:d_inner + (c + 1) * col_tile])
        w = [convw_ref[k:k + 1, sl] for k in range(D_CONV)]
        prev = pltpu.roll(cur, 1, axis=0)
        acc = pltpu.roll(w[0] * prev + w[1] * cur, 2, axis=0) + (w[2] * prev + w[3] * cur) + convb_ref[:, sl]
        edge[SUBLANES:2 * SUBLANES, sl] = cur[0:SUBLANES]
        head = convb_ref[:, sl] + w[tail] * cur[0:SUBLANES]
        for k in range(tail):
            r0 = SUBLANES - tail + k
            head = head + w[k] * edge[r0:r0 + SUBLANES, sl]
        edge[0:SUBLANES, sl] = cur[tm - SUBLANES:tm]
        act_ref[:, sl] = _silu(jnp.concatenate([head, acc[SUBLANES:tm]], axis=0)).astype(BF16)


def _inproj(h, g, w_all, conv_w, conv_b, seq_len):
    t, d = h.shape
    conv_dim = conv_w.shape[1]
    d_inner = w_all.shape[1] - conv_dim - LANES
    rows = lambda width: pl.BlockSpec((INPROJ_ROWS, width), lambda i: (i, 0))
    return pl.pallas_call(
        functools.partial(_inproj_kernel, steps_per_seq=seq_len // INPROJ_ROWS, col_tile=512),
        out_shape=(jax.ShapeDtypeStruct((t, d_inner), BF16),
                   jax.ShapeDtypeStruct((t, conv_dim), BF16),
                   jax.ShapeDtypeStruct((t, LANES), F32)),
        grid=(t // INPROJ_ROWS,),
        in_specs=[rows(d), _resident((1, d)), _resident(w_all.shape),
                  _resident((D_CONV, conv_dim)), _resident((1, conv_dim))],
        out_specs=(rows(d_inner), rows(conv_dim), rows(LANES)),
        scratch_shapes=[pltpu.VMEM((2 * SUBLANES, conv_dim), F32)],
        compiler_params=_params("arbitrary"),
        name="in_proj",
    )(h, g, w_all, conv_w, conv_b)


def _ssd_kernel(h_ref, z_ref, x_ref, bc_ref, dt_ref, dtb_ref, alog_ref, dskip_ref, gn_ref, expand_ref,
                wout_ref, out_ref, state, ybuf, ecum_x, wgt_x):
    L = SSD_CHUNK
    H = L // 2
    d_inner = x_ref.shape[1]
    n_heads = d_inner // SSM_HEAD_DIM
    heads_per_group = n_heads // SSM_GROUPS
    gw = heads_per_group * SSM_HEAD_DIM
    n_state = SSM_GROUPS * D_STATE

    @pl.when(pl.program_id(1) == 0)
    def _():
        state[...] = jnp.zeros(state.shape, F32)

    dt = _softplus(dt_ref[...] + dtb_ref[...])
    a = dt * (-LOG2_E * jnp.exp(alog_ref[...]))
    row = lax.broadcasted_iota(jnp.int32, (L, L), 0)
    col = lax.broadcasted_iota(jnp.int32, (L, L), 1)
    tri = jnp.where(col <= row, 1.0, 0.0).astype(BF16)
    a_hi = a.astype(BF16)
    r1 = a - a_hi.astype(F32)
    a_mid = r1.astype(BF16)
    a_lo = (r1 - a_mid.astype(F32)).astype(BF16)
    parts = _dot(tri, jnp.concatenate([a_hi, a_mid, a_lo], axis=1))
    cum = parts[:, 0:LANES] + parts[:, LANES:2 * LANES] + parts[:, 2 * LANES:3 * LANES]
    key_t = (cum - jnp.log2(dt)).T
    per_head = jnp.concatenate([jnp.exp2(cum), dt * jnp.exp2(cum[L - 1:L, :] - cum)], axis=0).astype(BF16)
    per_channel = _dot(per_head, expand_ref[...])
    ecum_x[...] = per_channel[0:L]
    wgt_x[...] = per_channel[L:2 * L]

    hrow = lax.broadcasted_iota(jnp.int32, (H, H), 0)
    hcol = lax.broadcasted_iota(jnp.int32, (H, H), 1)
    lower = hcol <= hrow
    glane = lax.broadcasted_iota(jnp.int32, (H, gw), 1)

    def own_channels(stacked):
        out = stacked[0:H]
        for j in range(1, heads_per_group):
            out = jnp.where(glane >= j * SSM_HEAD_DIM, stacked[j * H:(j + 1) * H], out)
        return out

    for g in range(SSM_GROUPS):
        gs = slice(g * gw, (g + 1) * gw)
        xb = x_ref[:, gs]
        x_g = xb.astype(F32)
        b_g = bc_ref[:, g * D_STATE:(g + 1) * D_STATE]
        c_g = bc_ref[:, n_state + g * D_STATE:n_state + (g + 1) * D_STATE]
        st = state[g]
        b_t = b_g.astype(F32).T.astype(BF16)
        both = _dot(c_g, jnp.concatenate([b_t, st.astype(BF16)], axis=1))
        cb = both[:, 0:L]
        tops, bots = [], []
        for j in range(heads_per_group):
            hd = g * heads_per_group + j
            q_i = cum[:, hd:hd + 1]
            k_j = key_t[hd:hd + 1, :]
            d_tl = jnp.exp2(jnp.where(lower, q_i[0:H] - k_j[:, 0:H], -jnp.inf)) * cb[0:H, 0:H]
            d_bl = jnp.exp2(q_i[H:L] - k_j[:, 0:H]) * cb[H:L, 0:H]
            d_br = jnp.exp2(jnp.where(lower, q_i[H:L] - k_j[:, H:L], -jnp.inf)) * cb[H:L, H:L]
            tops.append(d_tl.astype(BF16))
            bots.append(jnp.concatenate([d_bl, d_br], axis=1).astype(BF16))
        y_top = own_channels(_dot(jnp.concatenate(tops, axis=0), xb[0:H]))
        y_bot = own_channels(_dot(jnp.concatenate(bots, axis=0), xb))
        inter = both[:, L:L + gw] * ecum_x[:, gs]
        y_g = jnp.concatenate([y_top, y_bot], axis=0) + inter + dskip_ref[:, gs] * x_g
        xw = (x_g * wgt_x[:, gs]).astype(BF16)
        state[g] = st * ecum_x[L - 1:L, gs] + _dot(b_t, xw)
        yg = y_g * _silu(z_ref[:, gs].astype(F32))
        ybuf[:, gs] = (yg * _rms_scale(yg) * gn_ref[:, gs]).astype(BF16)

    out_ref[...] = h_ref[...] + _dot(ybuf[...], wout_ref[...])


def _ssd(h, z, act, dt_raw, dt_bias, a_log, d_skip, gnorm_g, w_out, batch):
    t, d = h.shape
    d_inner = w_out.shape[0]
    assert act.shape[1] == 2 * d_inner, "x and B|C column blocks are addressed with one block width"
    L = SSD_CHUNK
    chunks = t // batch // L
    gw = d_inner // SSM_GROUPS
    expand = (jnp.arange(LANES)[:, None] == jnp.arange(d_inner)[None, :] // SSM_HEAD_DIM).astype(BF16)
    rows = lambda b, c: (b * chunks + c, 0)
    return pl.pallas_call(
        _ssd_kernel,
        out_shape=jax.ShapeDtypeStruct((t, d), F32),
        grid=(batch, chunks),
        in_specs=[pl.BlockSpec((L, d), rows),
                  pl.BlockSpec((L, d_inner), rows),
                  pl.BlockSpec((L, d_inner), rows),
                  pl.BlockSpec((L, d_inner), lambda b, c: (b * chunks + c, 1)),
                  pl.BlockSpec((L, LANES), rows),
                  _resident((1, LANES)), _resident((1, LANES)),
                  _resident((1, d_inner)), _resident((1, d_inner)),
                  _resident((LANES, d_inner)), _resident((d_inner, d))],
        out_specs=pl.BlockSpec((L, d), rows),
        scratch_shapes=[pltpu.VMEM((SSM_GROUPS, D_STATE, gw), F32),
                        pltpu.VMEM((L, d_inner), BF16),
                        pltpu.VMEM((L, d_inner), F32),
                        pltpu.VMEM((L, d_inner), F32)],
        compiler_params=_params("arbitrary", "arbitrary"),
        name="ssd",
    )(h, z, act, act, dt_raw, dt_bias, a_log, d_skip, gnorm_g, expand, w_out)


def _mlp_body(h, g_ref, wup_ref, wdown_ref, out_ref):
    u = (h * _rms_scale(h) * g_ref[...]).astype(BF16)
    acc = h
    for c in range(wup_ref.shape[1] // FF_TILE):
        sl = slice(c * FF_TILE, (c + 1) * FF_TILE)
        a = jnp.maximum(_dot(u, wup_ref[:, sl]), 0.0)
        acc = acc + _dot((a * a).astype(BF16), wdown_ref[sl, :])
    out_ref[...] = acc


def _mlp_kernel(h_ref, g_ref, wup_ref, wdown_ref, out_ref):
    _mlp_body(h_ref[...], g_ref, wup_ref, wdown_ref, out_ref)


def _proj_mlp_kernel(h_ref, a_ref, wo_ref, g_ref, wup_ref, wdown_ref, out_ref):
    _mlp_body(h_ref[...] + _dot(a_ref[...], wo_ref[...]), g_ref, wup_ref, wdown_ref, out_ref)


def _mlp(h, g, w_up, w_down, layer, attn=None, w_o=None):
    t, d = h.shape
    ff = w_up.shape[2]
    rows = pl.BlockSpec((ROW_TILE, d), lambda i: (i, 0))
    stacked = lambda shape: pl.BlockSpec((None,) + shape, lambda i: (layer, 0, 0), pipeline_mode=pl.Buffered(1))
    weights = [_resident((1, d)), stacked((d, ff)), stacked((ff, d))]
    if attn is None:
        body, ins, specs = _mlp_kernel, (h, g, w_up, w_down), [rows] + weights
    else:
        body, ins = _proj_mlp_kernel, (h, attn, w_o, g, w_up, w_down)
        specs = [rows, rows, _resident(w_o.shape)] + weights
    return pl.pallas_call(
        body,
        out_shape=jax.ShapeDtypeStruct((t, d), F32),
        grid=(t // ROW_TILE,),
        in_specs=specs,
        out_specs=rows,
        compiler_params=_params("parallel"),
        name="mlp",
    )(*ins)


def _kvq_kernel(h_ref, gkv_ref, gq_ref, wkv_ref, wq_ref, gk_ref, q_ref, kt_ref, v_ref):
    x = h_ref[0]
    d = x.shape[1]
    xn = x * _rms_scale(x)
    ukv = (xn * gkv_ref[...]).astype(BF16)
    uq = (xn * gq_ref[...]).astype(BF16)
    k_t = _dot(ukv, wkv_ref[:, 0:d]).T
    q_ref[0] = _dot(uq, wq_ref[...]).astype(BF16)
    v_ref[0] = _dot(ukv, wkv_ref[:, d:2 * d]).astype(BF16)
    for hd in range(d // SB_HEAD_DIM):
        hs = slice(hd * SB_HEAD_DIM, (hd + 1) * SB_HEAD_DIM)
        kh = k_t[hs, :]
        scale = lax.rsqrt(jnp.mean(kh * kh, axis=0, keepdims=True) + EPS)
        kt_ref[0, hs, :] = (kh * scale * gk_ref[...]).astype(BF16)


def _kvq(h3, g_kv, g_q, w_kv, w_q, g_k):
    b, s, d = h3.shape
    rows = pl.BlockSpec((1, ROW_TILE, d), lambda bi, i: (bi, i, 0))
    return pl.pallas_call(
        _kvq_kernel,
        out_shape=(jax.ShapeDtypeStruct((b, s, d), BF16),
                   jax.ShapeDtypeStruct((b, d, s), BF16),
                   jax.ShapeDtypeStruct((b, s, d), BF16)),
        grid=(b, s // ROW_TILE),
        in_specs=[rows, _resident((1, d)), _resident((1, d)), _resident((d, 2 * d)),
                  _resident((d, d)), _resident((SB_HEAD_DIM, 1))],
        out_specs=(rows, pl.BlockSpec((1, d, ROW_TILE), lambda bi, i: (bi, 0, i)), rows),
        compiler_params=_params("parallel", "parallel"),
        name="kvq",
    )(h3, g_kv, g_q, w_kv, w_q, g_k)


def _attn_kernel(q_ref, kt_ref, v_ref, gq_ref, suffix_ref, o_ref, acc_ref, later_ref):
    R, KT = ATT_ROWS, ATT_KEYS
    width = q_ref.shape[2]
    heads = width // SB_HEAD_DIM
    q_tiles = q_ref.shape[1] // R
    step = pl.program_id(2)
    lane = lax.broadcasted_iota(jnp.int32, (R, width), 1)
    own = [jnp.logical_and(lane >= hd * SB_HEAD_DIM, lane < (hd + 1) * SB_HEAD_DIM) for hd in range(heads)]
    per_half = LANES // SB_HEAD_DIM
    row = lax.broadcasted_iota(jnp.int32, (heads * R, R), 0)
    col = lax.broadcasted_iota(jnp.int32, (heads * R, R), 1)
    causal = col < (row & (R - 1))

    def stacked_queries(sub):
        q = q_ref[0, sub * R:(sub + 1) * R, :].astype(F32)
        sq = q * q
        inv = jnp.zeros_like(q)
        for hd in range(heads):
            ms = jnp.sum(jnp.where(own[hd], sq, 0.0), axis=-1, keepdims=True) / SB_HEAD_DIM
            inv = jnp.where(own[hd], lax.rsqrt(ms + EPS), inv)
        qn = q * inv * gq_ref[...] * (LOG2_E / math.sqrt(SB_HEAD_DIM))
        return jnp.concatenate([jnp.where(own[hd], qn, 0.0).astype(BF16) for hd in range(heads)], axis=0)

    def chunk(qs, start, size, diagonal):
        z = _dot(qs, kt_ref[0, :, pl.ds(start, size)])
        if diagonal:
            z = jnp.where(causal, z, MASKED_LOGIT)
        sp = jnp.where(z > SOFTPLUS_LINEAR, z, jnp.log2(1.0 + jnp.exp2(z)))
        cum = _dot(sp.astype(BF16), suffix_ref[0:size, 0:size])
        p = jnp.exp2(z - cum).astype(BF16)
        pv = [_dot(p[hf * per_half * R:(hf + 1) * per_half * R],
                   v_ref[0, pl.ds(start, size), hf * LANES:(hf + 1) * LANES]) for hf in range(width // LANES)]
        return jnp.concatenate(pv, axis=0), cum[:, 0:1]

    qss, lives = [], []
    for sub in range(q_tiles):
        qi = step * q_tiles + sub
        qs = stacked_queries(sub)
        has_before = qi > 0
        before = pl.multiple_of(jnp.maximum(qi * R - KT, 0), R)
        pv, later = chunk(qs, pl.multiple_of(qi * R, R), R, True)
        pv_b, tot_b = chunk(qs, before, KT, False)
        acc_ref[sub] = pv + pv_b * jnp.where(has_before, jnp.exp2(-later), 0.0)
        later = later + jnp.where(has_before, tot_b, 0.0)
        later_ref[sub] = later
        qss.append(qs)
        lives.append(jnp.min(later))

    for sub in range(q_tiles):
        end0 = jnp.maximum((step * q_tiles + sub) * R - KT, 0)

        def cond(carry):
            end, live = carry
            return jnp.logical_and(end >= KT, live < ATT_DEAD_LOG2DECAY)

        def body(carry, sub=sub):
            end, _ = carry
            start = pl.multiple_of(end - KT, KT)
            pv, tot = chunk(qss[sub], start, KT, False)
            later = later_ref[sub]
            acc_ref[sub] += pv * jnp.exp2(-later)
            later = later + tot
            later_ref[sub] = later
            return start, jnp.min(later)

        lax.while_loop(cond, body, (end0, lives[sub]))

    half_lane = lax.broadcasted_iota(jnp.int32, (R, LANES), 1)
    for sub in range(q_tiles):
        for hf in range(width // LANES):
            out = acc_ref[sub, hf * per_half * R:(hf * per_half + 1) * R]
            for j in range(1, per_half):
                out = jnp.where(half_lane >= j * SB_HEAD_DIM,
                                acc_ref[sub, (hf * per_half + j) * R:(hf * per_half + j + 1) * R], out)
            o_ref[0, sub * R:(sub + 1) * R, hf * LANES:(hf + 1) * LANES] = out.astype(BF16)


def _attention(q, k_t, v, g_q):
    b, s, d = q.shape
    assert ATT_KEYS == ATT_ROWS, "the key walk left of the diagonal assumes chunk-aligned query tiles"
    rows = ATT_Q_TILES * ATT_ROWS
    width = ATT_HEADS * SB_HEAD_DIM
    idx = jnp.arange(ATT_KEYS)
    suffix = (idx[:, None] >= idx[None, :]).astype(BF16)
    return pl.pallas_call(
        _attn_kernel,
        out_shape=jax.ShapeDtypeStruct((b, s, d), BF16),
        grid=(b, d // width, s // rows),
        in_specs=[pl.BlockSpec((1, rows, width), lambda bi, hb, i: (bi, i, hb)),
                  pl.BlockSpec((1, width, s), lambda bi, hb, i: (bi, hb, 0)),
                  pl.BlockSpec((1, s, width), lambda bi, hb, i: (bi, 0, hb)),
                  _resident((1, width)), _resident((ATT_KEYS, ATT_KEYS))],
        out_specs=pl.BlockSpec((1, rows, width), lambda bi, hb, i: (bi, i, hb)),
        scratch_shapes=[pltpu.VMEM((ATT_Q_TILES, ATT_HEADS * ATT_ROWS, LANES), F32),
                        pltpu.VMEM((ATT_Q_TILES, ATT_HEADS * ATT_ROWS, 1), F32)],
        compiler_params=_params("parallel", "parallel", "arbitrary"),
        name="attn",
    )(q, k_t, v, g_q, suffix)


def kernel(x, a_norm_g, a_w_in, a_conv_w, a_conv_b, a_dt_bias, a_a_log, a_d_skip, a_gnorm_g, a_w_out,
           kv_norm_g, w_kv, k_norm_g, b_norm_g, b_w_q, b_q_norm_g, b_w_o, mlp_norm_g, w_up, w_down):
    b, s, d = x.shape
    n_a, n_b = a_w_in.shape[0], b_w_q.shape[0]
    n_heads = a_dt_bias.shape[1]

    def lane_pad(v):
        return jnp.pad(v, (0, LANES - v.shape[0]))[None, :]

    h = x.reshape(b * s, d)
    w_up_b, w_down_b = w_up.astype(BF16), w_down.astype(BF16)
    for l in range(n_a):
        w_all = jnp.pad(a_w_in[l], ((0, 0), (0, LANES - n_heads))).astype(BF16)
        z, act, dt_raw = _inproj(h, a_norm_g[l][None, :], w_all, a_conv_w[l], a_conv_b[l][None, :], s)
        h = _ssd(h, z, act, dt_raw, lane_pad(a_dt_bias[l]), lane_pad(a_a_log[l]),
                 jnp.repeat(a_d_skip[l], SSM_HEAD_DIM)[None, :], a_gnorm_g[l][None, :],
                 a_w_out[l].astype(BF16), b)
        h = _mlp(h, mlp_norm_g[l][None, :], w_up_b, w_down_b, l)
    q = k_t = v = None
    for j in range(n_b):
        l = n_a + j
        if j == 0:
            q, k_t, v = _kvq(h.reshape(b, s, d), kv_norm_g[None, :], b_norm_g[j][None, :],
                             w_kv.astype(BF16), b_w_q[j].astype(BF16), k_norm_g[:, None])
        else:
            raise NotImplementedError("one stick-breaking layer per shared K/V projection call")
        attn = _attention(q, k_t, v, jnp.tile(b_q_norm_g[j], ATT_HEADS)[None, :])
        h = _mlp(h, mlp_norm_g[l][None, :], w_up_b, w_down_b, l,
                 attn=attn.reshape(b * s, d), w_o=b_w_o[j].astype(BF16))
    return h.reshape(b, s, d)
```
